```python
import math
import jax, jax.numpy as jnp
from jax import lax
import numpy as np

D_MODEL = 1024
BATCH = 4
SEQ = 4096
DEPTH = 2

N_MEM = 256
HEAD_DIM = 64
SSD_HEADS = 6
SSD_WIDTH = SSD_HEADS * HEAD_DIM
SSD_GROUPS = 2
SSD_STATE = 64
SSD_CONV = 4
SSD_CHUNK = 128
SSD_CONV_CH = SSD_WIDTH + 2 * SSD_GROUPS * SSD_STATE
SB_HEADS = 6
SB_WIDTH = SB_HEADS * HEAD_DIM
SB_BLOCK = 128
MEM_HEADS = 4
MEM_WIDTH = MEM_HEADS * HEAD_DIM
D_MIX = SSD_WIDTH + SB_WIDTH + MEM_WIDTH
IN_SPLITS = [SSD_WIDTH, SSD_CONV_CH, SSD_HEADS, SB_WIDTH, SB_WIDTH, SB_WIDTH, MEM_WIDTH]
D_IN = sum(IN_SPLITS)
PEER_HEADS = 8
PEER_NKEYS = 128
PEER_EXPERTS = PEER_NKEYS * PEER_NKEYS
PEER_DKEY = 128
PEER_TOPK = 16
PEER_BLOCK = 128
EPS = 1e-6

kernel_name = "hymba_ssd_stickbreak_peer_trunk"


def rmsnorm(x, g):
    xf = x.astype(jnp.float32)
    y = xf * lax.rsqrt(jnp.mean(xf * xf, axis=-1, keepdims=True) + EPS)
    return (y * g.astype(jnp.float32)).astype(x.dtype)


def causal_depthwise_conv(x, w, b):
    k_w, ch = w.shape
    y = lax.conv_general_dilated(
        x, w.astype(x.dtype)[:, None, :], window_strides=(1,), padding=[(k_w - 1, 0)],
        dimension_numbers=("NWC", "WIO", "NWC"), feature_group_count=ch)
    return y + b.astype(x.dtype)


def ssd_scan(xh, dt, a, bm, cm):
    bsz, seqlen, nh, hp = xh.shape
    ns = bm.shape[-1]
    q = SSD_CHUNK
    nc = seqlen // q
    xdt = (xh * dt[..., None]).reshape(bsz, nc, q, nh, hp)
    adt = (dt * a).reshape(bsz, nc, q, nh)
    bm = bm.reshape(bsz, nc, q, nh, ns)
    cm = cm.reshape(bsz, nc, q, nh, ns)
    a_cs = jnp.cumsum(adt, axis=2)
    seg = a_cs[:, :, :, None, :] - a_cs[:, :, None, :, :]
    causal = jnp.tril(jnp.ones((q, q), dtype=bool))[None, None, :, :, None]
    lmat = jnp.exp(jnp.where(causal, seg, -jnp.inf))
    y_diag = jnp.einsum("bclhn,bcshn,bclsh,bcshp->bclhp", cm, bm, lmat, xdt)
    decay_s = jnp.exp(a_cs[:, :, -1:, :] - a_cs)
    states = jnp.einsum("bclhn,bclh,bclhp->bchpn", bm, decay_s, xdt)
    chunk_decay = jnp.exp(a_cs[:, :, -1, :])

    def step(h, inp):
        s_c, d_c = inp
        return h * d_c[:, :, None, None] + s_c, h

    h0 = jnp.zeros((bsz, nh, hp, ns), dtype=jnp.float32)
    _, h_in = lax.scan(step, h0, (jnp.swapaxes(states, 0, 1), jnp.swapaxes(chunk_decay, 0, 1)))
    h_in = jnp.swapaxes(h_in, 0, 1)
    y_off = jnp.einsum("bclhn,bchpn,bclh->bclhp", cm, h_in, jnp.exp(a_cs))
    return (y_diag + y_off).reshape(bsz, seqlen, nh, hp)


def ssd_mixer(u_z, u_xbc, u_dt, conv_w, conv_b, dt_bias, a_log, d_skip, norm_g):
    bsz, seqlen, _ = u_z.shape
    xbc = jax.nn.silu(causal_depthwise_conv(u_xbc, conv_w, conv_b))
    xs, bs, cs = jnp.split(xbc, [SSD_WIDTH, SSD_WIDTH + SSD_GROUPS * SSD_STATE], axis=-1)
    rep = SSD_HEADS // SSD_GROUPS
    xh = xs.reshape(bsz, seqlen, SSD_HEADS, HEAD_DIM).astype(jnp.float32)
    bm = jnp.repeat(bs.reshape(bsz, seqlen, SSD_GROUPS, SSD_STATE), rep, axis=2).astype(jnp.float32)
    cm = jnp.repeat(cs.reshape(bsz, seqlen, SSD_GROUPS, SSD_STATE), rep, axis=2).astype(jnp.float32)
    dt = jax.nn.softplus(u_dt.astype(jnp.float32) + dt_bias.astype(jnp.float32))
    a = -jnp.exp(a_log.astype(jnp.float32))
    y = ssd_scan(xh, dt, a, bm, cm) + xh * d_skip.astype(jnp.float32)[:, None]
    y = y.reshape(bsz, seqlen, SSD_WIDTH)
    return rmsnorm(y * jax.nn.silu(u_z.astype(jnp.float32)), norm_g)


def stick_breaking_attention(q, k, v):
    bsz, seqlen, nh, dh = q.shape
    q = jnp.transpose(q, (0, 2, 1, 3))
    k = jnp.transpose(k, (0, 2, 1, 3))
    v = jnp.transpose(v, (0, 2, 1, 3))
    scale = dh ** -0.5
    outs = []
    for i in range(seqlen // SB_BLOCK):
        start = i * SB_BLOCK
        end = start + SB_BLOCK
        qb = q[:, :, start:end]
        kp = k[:, :, :end]
        vp = v[:, :, :end]
        z = jnp.einsum("bhqd,bhkd->bhqk", qb, kp).astype(jnp.float32) * scale
        strict = jnp.arange(end)[None, :] < jnp.arange(start, end)[:, None]
        log_beta = jax.nn.log_sigmoid(z)
        log_keep = jnp.where(strict, jax.nn.log_sigmoid(-z), 0.0)
        suffix = lax.cumsum(log_keep, axis=3, reverse=True) - log_keep
        w = jnp.where(strict, jnp.exp(log_beta + suffix), 0.0)
        outs.append(jnp.einsum("bhqk,bhkd->bhqd", w.astype(vp.dtype), vp))
    out = jnp.concatenate(outs, axis=2)
    return jnp.transpose(out, (0, 2, 1, 3)).reshape(bsz, seqlen, nh * dh)


def memory_cross_attention(q, mk, mv, qg, kg):
    bsz, seqlen, nh, dh = q.shape
    q = rmsnorm(q, qg)
    mk = rmsnorm(mk, kg)
    s = jnp.einsum("bshd,bmhd->bhsm", q, mk).astype(jnp.float32) * (dh ** -0.5)
    p = jax.nn.softmax(s, axis=-1)
    o = jnp.einsum("bhsm,bmhd->bshd", p.astype(mv.dtype), mv)
    return o.reshape(bsz, seqlen, nh * dh)


def peer_ffn(h, wq, keys, u_tab, v_tab):
    bsz, seqlen, d = h.shape
    t = bsz * seqlen
    hf = h.reshape(t, d)
    q = (hf @ wq).reshape(t, PEER_HEADS, 2, PEER_DKEY // 2)
    scores = jnp.einsum("thcd,hcnd->thcn", q, keys).astype(jnp.float32)
    s_top, i_top = lax.top_k(scores, PEER_TOPK)
    cand = s_top[:, :, 0, :, None] + s_top[:, :, 1, None, :]
    cand_idx = i_top[:, :, 0, :, None] * PEER_NKEYS + i_top[:, :, 1, None, :]
    best, pos = lax.top_k(cand.reshape(t, PEER_HEADS, PEER_TOPK * PEER_TOPK), PEER_TOPK)
    eidx = jnp.take_along_axis(cand_idx.reshape(t, PEER_HEADS, PEER_TOPK * PEER_TOPK), pos, axis=-1)
    gates = jax.nn.softmax(best, axis=-1)

    def block(args):
        xb, ib, gb = args
        act = jax.nn.gelu(jnp.einsum("thkd,td->thk", u_tab[ib], xb).astype(jnp.float32), approximate=False)
        w = (gb * act).astype(xb.dtype)
        return jnp.einsum("thk,thkd->td", w, v_tab[ib])

    nb = t // PEER_BLOCK
    out = lax.map(block, (hf.reshape(nb, PEER_BLOCK, d),
                          eidx.reshape(nb, PEER_BLOCK, PEER_HEADS, PEER_TOPK),
                          gates.reshape(nb, PEER_BLOCK, PEER_HEADS, PEER_TOPK)))
    return out.reshape(bsz, seqlen, d)


def setup_inputs(seed: int = 0) -> dict:
    key = jax.random.key(seed)
    ks = jax.random.split(key, 24)
    f32 = jnp.float32
    L = DEPTH

    def nrm(k, shape, scale):
        return jax.random.normal(k, shape, dtype=f32) * scale

    def gain(k, shape):
        return 1.0 + 0.05 * jax.random.normal(k, shape, dtype=f32)

    dt0 = jnp.exp(jax.random.uniform(ks[5], (L, SSD_HEADS), dtype=f32,
                                     minval=math.log(1e-3), maxval=math.log(1e-1)))
    dt_bias = dt0 + jnp.log(-jnp.expm1(-dt0))
    a_log = jnp.log(jax.random.uniform(ks[6], (L, SSD_HEADS), dtype=f32, minval=1.0, maxval=16.0))
    return {
        "x": nrm(ks[0], (BATCH, SEQ, D_MODEL), 1.0),
        "mem": nrm(ks[1], (BATCH, N_MEM, D_MODEL), 1.0),
        "norm1_g": gain(ks[2], (L, D_MODEL)),
        "w_in": nrm(ks[3], (L, D_MODEL, D_IN), D_MODEL ** -0.5),
        "conv_w": nrm(ks[4], (L, SSD_CONV, SSD_CONV_CH), SSD_CONV ** -0.5),
        "conv_b": nrm(ks[7], (L, SSD_CONV_CH), 0.02),
        "dt_bias": dt_bias,
        "a_log": a_log,
        "d_skip": gain(ks[8], (L, SSD_HEADS)),
        "ssd_norm_g": gain(ks[9], (L, SSD_WIDTH)),
        "sb_out_g": gain(ks[10], (L, SB_WIDTH)),
        "mem_norm_g": gain(ks[11], (L, D_MODEL)),
        "w_mem_kv": nrm(ks[12], (L, D_MODEL, 2 * MEM_WIDTH), D_MODEL ** -0.5),
        "mq_norm_g": gain(ks[13], (L, HEAD_DIM)),
        "mk_norm_g": gain(ks[14], (L, HEAD_DIM)),
        "mem_out_g": gain(ks[15], (L, MEM_WIDTH)),
        "w_out": nrm(ks[16], (L, D_MIX, D_MODEL), D_MIX ** -0.5),
        "norm2_g": gain(ks[17], (L, D_MODEL)),
        "peer_wq": nrm(ks[18], (L, D_MODEL, PEER_HEADS * PEER_DKEY), D_MODEL ** -0.5),
        "peer_keys": nrm(ks[19], (L, PEER_HEADS, 2, PEER_NKEYS, PEER_DKEY // 2), (PEER_DKEY // 2) ** -0.5),
        "peer_u": nrm(ks[20], (L, PEER_EXPERTS, D_MODEL), D_MODEL ** -0.5),
        "peer_v": nrm(ks[21], (L, PEER_EXPERTS, D_MODEL), PEER_HEADS ** -0.5),
    }


def reference(x, mem, norm1_g, w_in, conv_w, conv_b, dt_bias, a_log, d_skip, ssd_norm_g,
              sb_out_g, mem_norm_g, w_mem_kv, mq_norm_g, mk_norm_g, mem_out_g, w_out,
              norm2_g, peer_wq, peer_keys, peer_u, peer_v):
    bsz, seqlen, _ = x.shape
    split_pts = list(np.cumsum(IN_SPLITS)[:-1])
    for l in range(DEPTH):
        h = rmsnorm(x, norm1_g[l])
        proj = h @ w_in[l]
        u_z, u_xbc, u_dt, q_sb, k_sb, v_sb, q_mem = jnp.split(proj, split_pts, axis=-1)

        y_ssd = ssd_mixer(u_z, u_xbc, u_dt, conv_w[l], conv_b[l], dt_bias[l], a_log[l],
                          d_skip[l], ssd_norm_g[l])

        hs = (bsz, seqlen, SB_HEADS, HEAD_DIM)
        y_sb = stick_breaking_attention(q_sb.reshape(hs), k_sb.reshape(hs), v_sb.reshape(hs))
        y_sb = rmsnorm(y_sb, sb_out_g[l])

        mem_kv = rmsnorm(mem, mem_norm_g[l]) @ w_mem_kv[l]
        mk, mv = jnp.split(mem_kv, 2, axis=-1)
        mhs = (bsz, N_MEM, MEM_HEADS, HEAD_DIM)
        y_mem = memory_cross_attention(q_mem.reshape(bsz, seqlen, MEM_HEADS, HEAD_DIM),
                                       mk.reshape(mhs), mv.reshape(mhs), mq_norm_g[l], mk_norm_g[l])
        y_mem = rmsnorm(y_mem, mem_out_g[l])

        y = jnp.concatenate([y_ssd.astype(x.dtype), y_sb.astype(x.dtype), y_mem.astype(x.dtype)], axis=-1) @ w_out[l]
        x = x + y.astype(x.dtype)

        h2 = rmsnorm(x, norm2_g[l])
        x = x + peer_ffn(h2, peer_wq[l], peer_keys[l], peer_u[l], peer_v[l]).astype(x.dtype)
    return x
```

```python
import functools
import math

import jax
import jax.numpy as jnp
from jax import lax
from jax.experimental import pallas as pl
from jax.experimental.pallas import tpu as pltpu

F32 = jnp.float32
BF16 = jnp.bfloat16

D_MODEL = 1024
HEAD_DIM = 64
SSD_HEADS = 6
SSD_WIDTH = SSD_HEADS * HEAD_DIM
SSD_GROUPS = 2
SSD_STATE = 64
SSD_CONV = 4
SSD_CHUNK = 128
SSD_CONV_CH = SSD_WIDTH + 2 * SSD_GROUPS * SSD_STATE
SB_HEADS = 6
SB_WIDTH = SB_HEADS * HEAD_DIM
SB_BLOCK = 128
MEM_HEADS = 4
MEM_WIDTH = MEM_HEADS * HEAD_DIM
IN_SPLITS = [SSD_WIDTH, SSD_CONV_CH, SSD_HEADS, SB_WIDTH, SB_WIDTH, SB_WIDTH, MEM_WIDTH]
PEER_HEADS = 8
PEER_NKEYS = 128
PEER_EXPERTS = PEER_NKEYS * PEER_NKEYS
PEER_DKEY = 128
PEER_TOPK = 16
EPS = 1e-6

LANES = 128
DT_PAD = LANES
SSD_OUT = SSD_WIDTH + SSD_CONV_CH + DT_PAD
SB_OUT = 3 * SB_WIDTH
IN_COLS = SSD_OUT + SB_OUT + MEM_WIDTH
VMEM_LIMIT = 48 * 1024 * 1024

NEG_INF = float("-inf")


def _params(sem, vmem=None):
    return pltpu.CompilerParams(dimension_semantics=sem, vmem_limit_bytes=vmem)


def _rms(x, g):
    ms = jnp.mean(x * x, axis=-1, keepdims=True)
    return x * lax.rsqrt(ms + EPS) * g


def _sigmoid(x):
    return 1.0 / (1.0 + jnp.exp(-x))


def _dot(a, b):
    return jnp.dot(a, b, preferred_element_type=F32)


def _dot_nt(a, b):
    return lax.dot_general(a, b, (((1,), (1,)), ((), ())), preferred_element_type=F32)


def _split3(x):
    hi = x.astype(BF16)
    r1 = x - hi.astype(F32)
    mid = r1.astype(BF16)
    lo = (r1 - mid.astype(F32)).astype(BF16)
    return hi, mid, lo


def _inproj_body(x_ref, g_ref, w_ref, ssd_ref, sb_ref, qm_ref):
    h = _rms(x_ref[...], g_ref[...]).astype(BF16)
    ssd_ref[...] = _dot(h, w_ref[:, 0:SSD_OUT])
    sb_ref[...] = _dot(h, w_ref[:, SSD_OUT:SSD_OUT + SB_OUT]).astype(BF16)
    qm_ref[...] = _dot(h, w_ref[:, SSD_OUT + SB_OUT:IN_COLS])


def _in_proj(x2, g, w, tm):
    t = x2.shape[0]
    return pl.pallas_call(
        _inproj_body,
        grid=(t // tm,),
        in_specs=[pl.BlockSpec((tm, D_MODEL), lambda i: (i, 0)),
                  pl.BlockSpec((1, D_MODEL), lambda i: (0, 0)),
                  pl.BlockSpec((D_MODEL, IN_COLS), lambda i: (0, 0))],
        out_specs=[pl.BlockSpec((tm, SSD_OUT), lambda i: (i, 0)),
                   pl.BlockSpec((tm, SB_OUT), lambda i: (i, 0)),
                   pl.BlockSpec((tm, MEM_WIDTH), lambda i: (i, 0))],
        out_shape=[jax.ShapeDtypeStruct((t, SSD_OUT), F32),
                   jax.ShapeDtypeStruct((t, SB_OUT), BF16),
                   jax.ShapeDtypeStruct((t, MEM_WIDTH), F32)],
        compiler_params=_params(("parallel",), VMEM_LIMIT),
        name="in_proj",
    )(x2, g, w)


def _ssd_body(u_ref, cw_ref, cb_ref, dtb_ref, alog_ref, dsk_ref, ng_ref, y_ref, xpad_ref, st_ref):
    q_len = SSD_CHUNK
    c = pl.program_id(1)

    @pl.when(c == 0)
    def _():
        xpad_ref[0:8, :] = jnp.zeros((8, SSD_CONV_CH), F32)
        st_ref[...] = jnp.zeros(st_ref.shape, F32)

    xpad_ref[8:8 + q_len, :] = u_ref[:, SSD_WIDTH:SSD_WIDTH + SSD_CONV_CH]
    conv = cb_ref[...]
    for k in range(SSD_CONV):
        off = 8 - (SSD_CONV - 1) + k
        conv = conv + cw_ref[k:k + 1, :] * xpad_ref[off:off + q_len, :]
    xpad_ref[0:8, :] = xpad_ref[q_len:q_len + 8, :]
    xbc = conv * _sigmoid(conv)

    xs = xbc[:, 0:SSD_WIDTH]
    bm = xbc[:, SSD_WIDTH:SSD_WIDTH + LANES]
    cm = xbc[:, SSD_WIDTH + LANES:SSD_WIDTH + 2 * LANES]

    udt = u_ref[:, SSD_WIDTH + SSD_CONV_CH:SSD_OUT] + dtb_ref[...]
    dt = jnp.maximum(udt, 0.0) + jnp.log1p(jnp.exp(-jnp.abs(udt)))
    adt = dt * (-jnp.exp(alog_ref[...]))

    row = lax.broadcasted_iota(jnp.int32, (q_len, q_len), 0)
    lane = lax.broadcasted_iota(jnp.int32, (q_len, q_len), 1)
    causal = row >= lane
    lo_half = lane < HEAD_DIM
    tri = jnp.where(causal, 1.0, 0.0).astype(BF16)
    hi, mid, lo = _split3(adt)
    acs = _dot(tri, hi) + _dot(tri, mid) + _dot(tri, lo)
    acs_t = acs.T
    last = acs[q_len - 1:q_len, :]
    dec_s = jnp.exp(last - acs)
    eacs = jnp.exp(acs)
    cdec = jnp.exp(last)

    bm_b = bm.astype(BF16)
    bt_b = bm.T.astype(BF16)
    cmask = [jnp.where(lo_half, cm, 0.0).astype(BF16), jnp.where(lo_half, 0.0, cm).astype(BF16)]
    cb = [_dot_nt(cmask[g], bm_b) for g in range(SSD_GROUPS)]

    def col_pair(m, h0, h1):
        return jnp.where(lo_half, m[:, h0:h0 + 1], m[:, h1:h1 + 1])

    def lmat(h):
        seg = acs[:, h:h + 1] - acs_t[h:h + 1, :]
        return jnp.exp(jnp.where(causal, seg, NEG_INF))

    heads_per_group = SSD_HEADS // SSD_GROUPS
    ys = []
    for p in range(SSD_HEADS // 2):
        h0, h1 = 2 * p, 2 * p + 1
        g0, g1 = h0 // heads_per_group, h1 // heads_per_group
        xp = xs[:, p * LANES:(p + 1) * LANES]
        xdt = xp * col_pair(dt, h0, h1)
        xdt_b = xdt.astype(BF16)
        m0 = (cb[g0] * lmat(h0)).astype(BF16)
        m1 = (cb[g1] * lmat(h1)).astype(BF16)
        y_diag = jnp.where(lo_half, _dot(m0, xdt_b), _dot(m1, xdt_b))
        st = st_ref[p]
        st_b = st.astype(BF16)
        if g0 == g1:
            y_off = _dot(cmask[g0], st_b)
        else:
            y_off = jnp.where(lo_half, _dot(cmask[g0], st_b), _dot(cmask[g1], st_b))
        y_off = y_off * col_pair(eacs, h0, h1)
        dx = (xdt * col_pair(dec_s, h0, h1)).astype(BF16)
        st_ref[p] = st * col_pair(cdec, h0, h1) + _dot(bt_b, dx)
        ys.append(y_diag + y_off + xp * dsk_ref[:, p * LANES:(p + 1) * LANES])
    y = jnp.concatenate(ys, axis=1)
    z = u_ref[:, 0:SSD_WIDTH]
    y_ref[...] = _rms(y * (z * _sigmoid(z)), ng_ref[...]).astype(BF16)


def _ssd(u3, conv_w, conv_b, dt_bias, a_log, d_skip_cols, norm_g):
    b, s, _ = u3.shape
    nc = s // SSD_CHUNK
    const = lambda shape: pl.BlockSpec(shape, lambda i, j: (0,) * len(shape))
    return pl.pallas_call(
        _ssd_body,
        grid=(b, nc),
        in_specs=[pl.BlockSpec((None, SSD_CHUNK, SSD_OUT), lambda i, j: (i, j, 0)),
                  const((SSD_CONV, SSD_CONV_CH)), const((1, SSD_CONV_CH)),
                  const((1, DT_PAD)), const((1, DT_PAD)),
                  const((1, SSD_WIDTH)), const((1, SSD_WIDTH))],
        out_specs=pl.BlockSpec((None, SSD_CHUNK, SSD_WIDTH), lambda i, j: (i, j, 0)),
        out_shape=jax.ShapeDtypeStruct((b, s, SSD_WIDTH), BF16),
        scratch_shapes=[pltpu.VMEM((SSD_CHUNK + 8, SSD_CONV_CH), F32),
                        pltpu.VMEM((SSD_HEADS // 2, LANES, LANES), F32)],
        compiler_params=_params(("parallel", "arbitrary")),
        name="ssd",
    )(u3, conv_w, conv_b, dt_bias, a_log, d_skip_cols, norm_g)


def _sb_body(q_ref, k_ref, v_ref, o_ref):
    blk = SB_BLOCK
    i = pl.program_id(2)
    row = lax.broadcasted_iota(jnp.int32, (blk, blk), 0)
    lane = lax.broadcasted_iota(jnp.int32, (blk, blk), 1)
    lo_half = lane < HEAD_DIM
    strict = lane < row
    upper = jnp.where(row > lane, 1.0, 0.0).astype(BF16)
    ucat = jnp.concatenate([upper, upper], axis=0)
    q = q_ref[...]
    zero = jnp.zeros_like(q)
    qh = [jnp.where(lo_half, q, zero), jnp.where(lo_half, zero, q)]
    scale = HEAD_DIM ** -0.5

    def block(j, carry, diag):
        r0, r1, acc = carry
        start = pl.multiple_of(j * blk, blk)
        kb = k_ref[pl.ds(start, blk), :]
        vb = v_ref[pl.ds(start, blk), :]
        outs = []
        rs = []
        for hh, r in ((0, r0), (1, r1)):
            z = _dot_nt(qh[hh], kb) * scale
            log_beta = jnp.minimum(z, 0.0) - jnp.log1p(jnp.exp(-jnp.abs(z)))
            log_keep = log_beta - z
            if diag:
                log_keep = jnp.where(strict, log_keep, 0.0)
            hi = log_keep.astype(BF16)
            lo = (log_keep - hi.astype(F32)).astype(BF16)
            within = _dot(jnp.concatenate([hi, lo], axis=1), ucat)
            w = jnp.exp(log_beta + within + r)
            if diag:
                w = jnp.where(strict, w, 0.0)
            outs.append(_dot(w.astype(BF16), vb))
            rs.append(r + jnp.sum(log_keep, axis=-1, keepdims=True))
        return rs[0], rs[1], acc + jnp.where(lo_half, outs[0], outs[1])

    zeros = jnp.zeros((blk, blk), F32)
    carry = block(i, (zeros, zeros, zeros), True)
    carry = lax.fori_loop(0, i, lambda t, cr: block(i - 1 - t, cr, False), carry)
    o_ref[...] = carry[2]


def _sb_attn(sb3):
    b, s, _ = sb3.shape
    npair = SB_HEADS // 2
    nq = s // SB_BLOCK
    return pl.pallas_call(
        _sb_body,
        grid=(b, npair, nq),
        in_specs=[pl.BlockSpec((None, SB_BLOCK, LANES), lambda bi, p, i: (bi, i, p)),
                  pl.BlockSpec((None, s, LANES), lambda bi, p, i: (bi, 0, npair + p)),
                  pl.BlockSpec((None, s, LANES), lambda bi, p, i: (bi, 0, 2 * npair + p))],
        out_specs=pl.BlockSpec((None, SB_BLOCK, LANES), lambda bi, p, i: (bi, i, p)),
        out_shape=jax.ShapeDtypeStruct((b, s, SB_WIDTH), F32),
        compiler_params=_params(("parallel", "parallel", "arbitrary")),
        name="sb_attn",
    )(sb3, sb3, sb3)


def _head_rms(x, g):
    lane = lax.broadcasted_iota(jnp.int32, x.shape, 1)
    sq = x * x
    inv = jnp.zeros_like(x)
    for h in range(MEM_HEADS):
        in_head = (lane >= h * HEAD_DIM) & (lane < (h + 1) * HEAD_DIM)
        ms = jnp.sum(jnp.where(in_head, sq, 0.0), axis=-1, keepdims=True) * (1.0 / HEAD_DIM)
        inv = jnp.where(in_head, lax.rsqrt(ms + EPS), inv)
    return x * inv * g


def _memkv_body(m_ref, g_ref, w_ref, kg_ref, k_ref, v_ref):
    h = _rms(m_ref[...], g_ref[...]).astype(BF16)
    kv = _dot(h, w_ref[...])
    k_ref[...] = _head_rms(kv[:, 0:MEM_WIDTH], kg_ref[...]).astype(BF16)
    v_ref[...] = kv[:, MEM_WIDTH:2 * MEM_WIDTH].astype(BF16)


def _mem_kv(mem, g, w, kg_cols):
    b, m, _ = mem.shape
    return pl.pallas_call(
        _memkv_body,
        grid=(b,),
        in_specs=[pl.BlockSpec((None, m, D_MODEL), lambda i: (i, 0, 0)),
                  pl.BlockSpec((1, D_MODEL), lambda i: (0, 0)),
                  pl.BlockSpec((D_MODEL, 2 * MEM_WIDTH), lambda i: (0, 0)),
                  pl.BlockSpec((1, MEM_WIDTH), lambda i: (0, 0))],
        out_specs=[pl.BlockSpec((None, m, MEM_WIDTH), lambda i: (i, 0, 0)),
                   pl.BlockSpec((None, m, MEM_WIDTH), lambda i: (i, 0, 0))],
        out_shape=[jax.ShapeDtypeStruct((b, m, MEM_WIDTH), BF16),
                   jax.ShapeDtypeStruct((b, m, MEM_WIDTH), BF16)],
        compiler_params=_params(("parallel",)),
        name="mem_kv",
    )(mem, g, w, kg_cols)


def _memattn_body(q_ref, k_ref, v_ref, qg_ref, og_ref, o_ref):
    qn = _head_rms(q_ref[...], qg_ref[...]).astype(BF16)
    lane = lax.broadcasted_iota(jnp.int32, qn.shape, 1)
    k = k_ref[...]
    v = v_ref[...]
    out = jnp.zeros(qn.shape, F32)
    for h in range(MEM_HEADS):
        in_head = (lane >= h * HEAD_DIM) & (lane < (h + 1) * HEAD_DIM)
        s = _dot_nt(jnp.where(in_head, qn, jnp.zeros_like(qn)), k) * (HEAD_DIM ** -0.5)
        e = jnp.exp(s - jnp.max(s, axis=-1, keepdims=True))
        p = e / jnp.sum(e, axis=-1, keepdims=True)
        out = jnp.where(in_head, _dot(p.astype(BF16), v), out)
    o_ref[...] = _rms(out, og_ref[...]).astype(BF16)


def _mem_attn(qm3, mk, mv, qg_cols, og, tq):
    b, s, _ = qm3.shape
    m = mk.shape[1]
    return pl.pallas_call(
        _memattn_body,
        grid=(b, s // tq),
        in_specs=[pl.BlockSpec((None, tq, MEM_WIDTH), lambda i, j: (i, j, 0)),
                  pl.BlockSpec((None, m, MEM_WIDTH), lambda i, j: (i, 0, 0)),
                  pl.BlockSpec((None, m, MEM_WIDTH), lambda i, j: (i, 0, 0)),
                  pl.BlockSpec((1, MEM_WIDTH), lambda i, j: (0, 0)),
                  pl.BlockSpec((1, MEM_WIDTH), lambda i, j: (0, 0))],
        out_specs=pl.BlockSpec((None, tq, MEM_WIDTH), lambda i, j: (i, j, 0)),
        out_shape=jax.ShapeDtypeStruct((b, s, MEM_WIDTH), BF16),
        compiler_params=_params(("parallel", "parallel")),
        name="mem_attn",
    )(qm3, mk, mv, qg_cols, og)


def _outproj_body(x_ref, ys_ref, yb_ref, ym_ref, sbg_ref, w_ref, o_ref):
    yb = _rms(yb_ref[...], sbg_ref[...]).astype(BF16)
    acc = _dot(ys_ref[...], w_ref[0:SSD_WIDTH, :])
    acc = acc + _dot(yb, w_ref[SSD_WIDTH:SSD_WIDTH + SB_WIDTH, :])
    acc = acc + _dot(ym_ref[...], w_ref[SSD_WIDTH + SB_WIDTH:D_MODEL, :])
    o_ref[...] = x_ref[...] + acc


def _out_proj(x2, y_ssd, y_sb, y_mem, sb_g, w, tm):
    t = x2.shape[0]
    rows = lambda width: pl.BlockSpec((tm, width), lambda i: (i, 0))
    return pl.pallas_call(
        _outproj_body,
        grid=(t // tm,),
        in_specs=[rows(D_MODEL), rows(SSD_WIDTH), rows(SB_WIDTH), rows(MEM_WIDTH),
                  pl.BlockSpec((1, SB_WIDTH), lambda i: (0, 0)),
                  pl.BlockSpec((D_MODEL, D_MODEL), lambda i: (0, 0))],
        out_specs=rows(D_MODEL),
        out_shape=jax.ShapeDtypeStruct((t, D_MODEL), F32),
        compiler_params=_params(("parallel",), VMEM_LIMIT),
        name="out_proj",
    )(x2, y_ssd, y_sb, y_mem, sb_g, w)


def _top_values(scores, k):
    work = scores
    vals = []
    for r in range(k):
        m = jnp.max(work, axis=0, keepdims=True)
        vals.append(m)
        if r + 1 < k:
            work = jnp.where(work == m, NEG_INF, work)
    return vals


def _peer_pre_body(x_ref, g_ref, wqt_ref, keys_ref, h2t_ref, info_ref):
    topk = PEER_TOPK
    half = PEER_DKEY // 2
    h2 = _rms(x_ref[...], g_ref[...])
    h2t = h2.T.astype(BF16)
    h2t_ref[...] = h2t
    qt = _dot(wqt_ref[...], h2t).astype(BF16)
    for h in range(PEER_HEADS):
        s1 = _dot(keys_ref[h, 0], qt[(2 * h) * half:(2 * h + 1) * half, :])
        s2 = _dot(keys_ref[h, 1], qt[(2 * h + 1) * half:(2 * h + 2) * half, :])
        a = _top_values(s1, topk)
        b = _top_values(s2, topk)
        cands = [a[p] + b[q] for p in range(topk) for q in range(topk // (p + 1))]
        pad = (-len(cands)) % 8
        cand = jnp.concatenate(cands + [jnp.full_like(a[0], NEG_INF)] * pad, axis=0)
        tau = _top_values(cand, topk)[topk - 1]
        cnt = jnp.zeros_like(s1)
        rank = jnp.zeros_like(s2)
        for q in range(topk):
            cnt = cnt + jnp.where(s1 + b[q] >= tau, 1.0, 0.0)
            rank = rank + jnp.where(b[q] > s2, 1.0, 0.0)
        cnt = jnp.where(s1 >= a[topk - 1], cnt, 0.0)
        rank = jnp.where(s2 >= b[topk - 1], rank, 2.0 * topk)
        b_sorted = jnp.concatenate(b, axis=0)
        eb_sorted = jnp.exp(b_sorted - b[0])
        zsum = jnp.zeros_like(tau)
        for p in range(topk):
            sel = (a[p] + b_sorted) >= tau
            zsum = zsum + jnp.exp(a[p] - a[0]) * jnp.sum(jnp.where(sel, eb_sorted, 0.0), axis=0, keepdims=True)
        info_ref[0, h] = cnt
        info_ref[1, h] = jnp.exp(s1 - a[0])
        info_ref[2, h] = rank
        info_ref[3, h] = jnp.exp(s2 - b[0]) / zsum


def _peer_pre(x2, g, wqt, keys, tb):
    t = x2.shape[0]
    return pl.pallas_call(
        _peer_pre_body,
        grid=(t // tb,),
        in_specs=[pl.BlockSpec((tb, D_MODEL), lambda i: (i, 0)),
                  pl.BlockSpec((1, D_MODEL), lambda i: (0, 0)),
                  pl.BlockSpec((D_MODEL, D_MODEL), lambda i: (0, 0)),
                  pl.BlockSpec((PEER_HEADS, 2, PEER_NKEYS, PEER_DKEY // 2), lambda i: (0, 0, 0, 0))],
        out_specs=[pl.BlockSpec((D_MODEL, tb), lambda i: (0, i)),
                   pl.BlockSpec((4, PEER_HEADS, PEER_NKEYS, tb), lambda i: (0, 0, 0, i))],
        out_shape=[jax.ShapeDtypeStruct((D_MODEL, t), BF16),
                   jax.ShapeDtypeStruct((4, PEER_HEADS, PEER_NKEYS, t), F32)],
        compiler_params=_params(("parallel",), VMEM_LIMIT),
        name="peer_pre",
    )(x2, g, wqt, keys)


def _peer_main_body(h2t_ref, info_ref, u_ref, vt_ref, x_ref, o_ref, acc_ref, pt_ref, *, eb):
    e = pl.program_id(1)
    sub = eb // PEER_NKEYS

    @pl.when(e == 0)
    def _():
        acc_ref[...] = jnp.zeros(acc_ref.shape, F32)

    at = _dot(u_ref[...], h2t_ref[...])
    act = 0.5 * at * (1.0 + lax.erf(at * math.sqrt(0.5)))
    for r in range(sub):
        i = e * sub + r
        wd = jnp.zeros((PEER_NKEYS, at.shape[1]), F32)
        for h in range(PEER_HEADS):
            cnt = info_ref[0, h, pl.ds(i, 1), :]
            ea = info_ref[1, h, pl.ds(i, 1), :]
            wd = wd + jnp.where(info_ref[2, h] < cnt, info_ref[3, h], 0.0) * ea
        pt_ref[r * PEER_NKEYS:(r + 1) * PEER_NKEYS, :] = (
            wd * act[r * PEER_NKEYS:(r + 1) * PEER_NKEYS, :]).astype(BF16)
    acc_ref[...] += _dot(vt_ref[...], pt_ref[...])

    @pl.when(e == pl.num_programs(1) - 1)
    def _():
        o_ref[...] = x_ref[...] + acc_ref[...].T


def _peer_main(x2, h2t, info, u_b, vt_b, tb, eb):
    t = x2.shape[0]
    return pl.pallas_call(
        functools.partial(_peer_main_body, eb=eb),
        grid=(t // tb, PEER_EXPERTS // eb),
        in_specs=[pl.BlockSpec((D_MODEL, tb), lambda i, e: (0, i)),
                  pl.BlockSpec((4, PEER_HEADS, PEER_NKEYS, tb), lambda i, e: (0, 0, 0, i)),
                  pl.BlockSpec((eb, D_MODEL), lambda i, e: (e, 0)),
                  pl.BlockSpec((D_MODEL, eb), lambda i, e: (0, e)),
                  pl.BlockSpec((tb, D_MODEL), lambda i, e: (i, 0))],
        out_specs=pl.BlockSpec((tb, D_MODEL), lambda i, e: (i, 0)),
        out_shape=jax.ShapeDtypeStruct((t, D_MODEL), F32),
        scratch_shapes=[pltpu.VMEM((D_MODEL, tb), F32), pltpu.VMEM((eb, tb), BF16)],
        compiler_params=_params(("parallel", "arbitrary"), VMEM_LIMIT),
        name="peer_main",
    )(h2t, info, u_b, vt_b, x2)


def _pick(n, pref):
    return pref if n % pref == 0 else n


def _permute_w_in(w):
    pts = [0]
    for width in IN_SPLITS:
        pts.append(pts[-1] + width)
    z, xbc, dtw, q, k, v, qm = [w[:, pts[n]:pts[n + 1]] for n in range(len(IN_SPLITS))]
    dtw = jnp.pad(dtw, ((0, 0), (0, DT_PAD - SSD_HEADS)))
    return jnp.concatenate([z, xbc, dtw, q, k, v, qm], axis=1).astype(BF16)


def _pad_row(v, width):
    return jnp.pad(v, (0, width - v.shape[0])).reshape(1, width)


def kernel(x, mem, norm1_g, w_in, conv_w, conv_b, dt_bias, a_log, d_skip, ssd_norm_g, sb_out_g, mem_norm_g,
           w_mem_kv, mq_norm_g, mk_norm_g, mem_out_g, w_out, norm2_g, peer_wq, peer_keys, peer_u, peer_v):
    bsz, seqlen, _ = x.shape
    t = bsz * seqlen
    depth = w_in.shape[0]
    tm = _pick(t, 512)
    tq = _pick(seqlen, 512)
    tb = _pick(t, 512)
    eb = 512
    row = lambda v: v.reshape(1, -1)
    x2 = x.reshape(t, D_MODEL)
    for l in range(depth):
        u_ssd, u_sb, u_qm = _in_proj(x2, row(norm1_g[l]), _permute_w_in(w_in[l]), tm)
        y_ssd = _ssd(u_ssd.reshape(bsz, seqlen, SSD_OUT), conv_w[l], row(conv_b[l]),
                     _pad_row(dt_bias[l], DT_PAD), _pad_row(a_log[l], DT_PAD),
                     row(jnp.repeat(d_skip[l], HEAD_DIM)), row(ssd_norm_g[l]))
        y_sb = _sb_attn(u_sb.reshape(bsz, seqlen, SB_OUT))
        mk, mv = _mem_kv(mem, row(mem_norm_g[l]), w_mem_kv[l].astype(BF16),
                         row(jnp.tile(mk_norm_g[l], MEM_HEADS)))
        y_mem = _mem_attn(u_qm.reshape(bsz, seqlen, MEM_WIDTH), mk, mv,
                          row(jnp.tile(mq_norm_g[l], MEM_HEADS)), row(mem_out_g[l]), tq)
        x2 = _out_proj(x2, y_ssd.reshape(t, SSD_WIDTH), y_sb.reshape(t, SB_WIDTH),
                       y_mem.reshape(t, MEM_WIDTH), row(sb_out_g[l]), w_out[l].astype(BF16), tm)
        h2t, info = _peer_pre(x2, row(norm2_g[l]), peer_wq[l].T.astype(BF16), peer_keys[l].astype(BF16), tb)
        x2 = _peer_main(x2, h2t, info, peer_u[l].astype(BF16), peer_v[l].T.astype(BF16), tb, eb)
    return x2.reshape(bsz, seqlen, D_MODEL)
```

```python
import functools
import math

import jax
import jax.numpy as jnp
from jax import lax
from jax.experimental import pallas as pl
from jax.experimental.pallas import tpu as pltpu

F32 = jnp.float32
BF16 = jnp.bfloat16

D_MODEL = 1024
HEAD_DIM = 64
SSD_HEADS = 6
SSD_WIDTH = SSD_HEADS * HEAD_DIM
SSD_GROUPS = 2
SSD_STATE = 64
SSD_CONV = 4
SSD_CHUNK = 128
SSD_CONV_CH = SSD_WIDTH + 2 * SSD_GROUPS * SSD_STATE
SB_HEADS = 6
SB_WIDTH = SB_HEADS * HEAD_DIM
SB_BLOCK = 128
MEM_HEADS = 4
MEM_WIDTH = MEM_HEADS * HEAD_DIM
IN_SPLITS = [SSD_WIDTH, SSD_CONV_CH, SSD_HEADS, SB_WIDTH, SB_WIDTH, SB_WIDTH, MEM_WIDTH]
PEER_HEADS = 8
PEER_NKEYS = 128
PEER_EXPERTS = PEER_NKEYS * PEER_NKEYS
PEER_DKEY = 128
PEER_TOPK = 16
EPS = 1e-6

LANES = 128
DT_PAD = LANES
SSD_OUT = SSD_WIDTH + SSD_CONV_CH + DT_PAD
SB_OUT = 3 * SB_WIDTH
IN_COLS = SSD_OUT + SB_OUT + MEM_WIDTH
VMEM_LIMIT = 48 * 1024 * 1024

NEG_INF = float("-inf")


def _params(sem, vmem=None):
    return pltpu.CompilerParams(dimension_semantics=sem, vmem_limit_bytes=vmem)


def _rms(x, g):
    ms = jnp.mean(x * x, axis=-1, keepdims=True)
    return x * lax.rsqrt(ms + EPS) * g


def _sigmoid(x):
    return 1.0 / (1.0 + jnp.exp(-x))


def _dot(a, b):
    return jnp.dot(a, b, preferred_element_type=F32)


def _dot_nt(a, b):
    return lax.dot_general(a, b, (((1,), (1,)), ((), ())), preferred_element_type=F32)


def _split3(x):
    hi = x.astype(BF16)
    r1 = x - hi.astype(F32)
    mid = r1.astype(BF16)
    lo = (r1 - mid.astype(F32)).astype(BF16)
    return hi, mid, lo


def _inproj_body(x_ref, g_ref, w_ref, ssd_ref, sb_ref, qm_ref):
    h = _rms(x_ref[...], g_ref[...]).astype(BF16)
    ssd_ref[...] = _dot(h, w_ref[:, 0:SSD_OUT])
    sb_ref[...] = _dot(h, w_ref[:, SSD_OUT:SSD_OUT + SB_OUT]).astype(BF16)
    qm_ref[...] = _dot(h, w_ref[:, SSD_OUT + SB_OUT:IN_COLS])


def _in_proj(x2, g, w, tm):
    t = x2.shape[0]
    return pl.pallas_call(
        _inproj_body,
        grid=(t // tm,),
        in_specs=[pl.BlockSpec((tm, D_MODEL), lambda i: (i, 0)),
                  pl.BlockSpec((1, D_MODEL), lambda i: (0, 0)),
                  pl.BlockSpec((D_MODEL, IN_COLS), lambda i: (0, 0))],
        out_specs=[pl.BlockSpec((tm, SSD_OUT), lambda i: (i, 0)),
                   pl.BlockSpec((tm, SB_OUT), lambda i: (i, 0)),
                   pl.BlockSpec((tm, MEM_WIDTH), lambda i: (i, 0))],
        out_shape=[jax.ShapeDtypeStruct((t, SSD_OUT), F32),
                   jax.ShapeDtypeStruct((t, SB_OUT), BF16),
                   jax.ShapeDtypeStruct((t, MEM_WIDTH), F32)],
        compiler_params=_params(("parallel",), VMEM_LIMIT),
        name="in_proj",
    )(x2, g, w)


def _ssd_body(u_ref, cw_ref, cb_ref, dtb_ref, alog_ref, dsk_ref, ng_ref, y_ref, xpad_ref, st_ref):
    q_len = SSD_CHUNK
    c = pl.program_id(1)

    @pl.when(c == 0)
    def _():
        xpad_ref[0:8, :] = jnp.zeros((8, SSD_CONV_CH), F32)
        st_ref[...] = jnp.zeros(st_ref.shape, F32)

    xpad_ref[8:8 + q_len, :] = u_ref[:, SSD_WIDTH:SSD_WIDTH + SSD_CONV_CH]
    conv = cb_ref[...]
    for k in range(SSD_CONV):
        off = 8 - (SSD_CONV - 1) + k
        conv = conv + cw_ref[k:k + 1, :] * xpad_ref[off:off + q_len, :]
    xpad_ref[0:8, :] = xpad_ref[q_len:q_len + 8, :]
    xbc = conv * _sigmoid(conv)

    xs = xbc[:, 0:SSD_WIDTH]
    bm = xbc[:, SSD_WIDTH:SSD_WIDTH + LANES]
    cm = xbc[:, SSD_WIDTH + LANES:SSD_WIDTH + 2 * LANES]

    udt = u_ref[:, SSD_WIDTH + SSD_CONV_CH:SSD_OUT] + dtb_ref[...]
    dt = jnp.maximum(udt, 0.0) + jnp.log1p(jnp.exp(-jnp.abs(udt)))
    adt = dt * (-jnp.exp(alog_ref[...]))

    row = lax.broadcasted_iota(jnp.int32, (q_len, q_len), 0)
    lane = lax.broadcasted_iota(jnp.int32, (q_len, q_len), 1)
    causal = row >= lane
    lo_half = lane < HEAD_DIM
    tri = jnp.where(causal, 1.0, 0.0).astype(BF16)
    hi, mid, lo = _split3(adt)
    acs = _dot(tri, hi) + _dot(tri, mid) + _dot(tri, lo)
    acs_t = acs.T
    last = acs[q_len - 1:q_len, :]
    dec_s = jnp.exp(last - acs)
    eacs = jnp.exp(acs)
    cdec = jnp.exp(last)

    bm_b = bm.astype(BF16)
    bt_b = bm.T.astype(BF16)
    cmask = [jnp.where(lo_half, cm, 0.0).astype(BF16), jnp.where(lo_half, 0.0, cm).astype(BF16)]
    cb = [_dot_nt(cmask[g], bm_b) for g in range(SSD_GROUPS)]

    def col_pair(m, h0, h1):
        return jnp.where(lo_half, m[:, h0:h0 + 1], m[:, h1:h1 + 1])

    def lmat(h):
        seg = acs[:, h:h + 1] - acs_t[h:h + 1, :]
        return jnp.exp(jnp.where(causal, seg, NEG_INF))

    heads_per_group = SSD_HEADS // SSD_GROUPS
    ys = []
    for p in range(SSD_HEADS // 2):
        h0, h1 = 2 * p, 2 * p + 1
        g0, g1 = h0 // heads_per_group, h1 // heads_per_group
        xp = xs[:, p * LANES:(p + 1) * LANES]
        xdt = xp * col_pair(dt, h0, h1)
        xdt_b = xdt.astype(BF16)
        m0 = (cb[g0] * lmat(h0)).astype(BF16)
        m1 = (cb[g1] * lmat(h1)).astype(BF16)
        y_diag = jnp.where(lo_half, _dot(m0, xdt_b), _dot(m1, xdt_b))
        st = st_ref[p]
        st_b = st.astype(BF16)
        if g0 == g1:
            y_off = _dot(cmask[g0], st_b)
        else:
            y_off = jnp.where(lo_half, _dot(cmask[g0], st_b), _dot(cmask[g1], st_b))
        y_off = y_off * col_pair(eacs, h0, h1)
        dx = (xdt * col_pair(dec_s, h0, h1)).astype(BF16)
        st_ref[p] = st * col_pair(cdec, h0, h1) + _dot(bt_b, dx)
        ys.append(y_diag + y_off + xp * dsk_ref[:, p * LANES:(p + 1) * LANES])
    y = jnp.concatenate(ys, axis=1)
    z = u_ref[:, 0:SSD_WIDTH]
    y_ref[...] = _rms(y * (z * _sigmoid(z)), ng_ref[...]).astype(BF16)


def _ssd(u3, conv_w, conv_b, dt_bias, a_log, d_skip_cols, norm_g):
    b, s, _ = u3.shape
    nc = s // SSD_CHUNK
    const = lambda shape: pl.BlockSpec(shape, lambda i, j: (0,) * len(shape))
    return pl.pallas_call(
        _ssd_body,
        grid=(b, nc),
        in_specs=[pl.BlockSpec((None, SSD_CHUNK, SSD_OUT), lambda i, j: (i, j, 0)),
                  const((SSD_CONV, SSD_CONV_CH)), const((1, SSD_CONV_CH)),
                  const((1, DT_PAD)), const((1, DT_PAD)),
                  const((1, SSD_WIDTH)), const((1, SSD_WIDTH))],
        out_specs=pl.BlockSpec((None, SSD_CHUNK, SSD_WIDTH), lambda i, j: (i, j, 0)),
        out_shape=jax.ShapeDtypeStruct((b, s, SSD_WIDTH), BF16),
        scratch_shapes=[pltpu.VMEM((SSD_CHUNK + 8, SSD_CONV_CH), F32),
                        pltpu.VMEM((SSD_HEADS // 2, LANES, LANES), F32)],
        compiler_params=_params(("parallel", "arbitrary")),
        name="ssd",
    )(u3, conv_w, conv_b, dt_bias, a_log, d_skip_cols, norm_g)


def _sb_body(q_ref, k_ref, v_ref, o_ref, acc_ref, r_ref, *, tile):
    m = pl.program_id(2)
    row = lax.broadcasted_iota(jnp.int32, (tile, tile), 0)
    col = lax.broadcasted_iota(jnp.int32, (tile, tile), 1)
    strict = col < row
    upper = jnp.where(row > col, 1.0, 0.0).astype(BF16)
    ucat = jnp.concatenate([upper, upper], axis=0)
    lo_half = lax.broadcasted_iota(jnp.int32, (tile, LANES), 1) < HEAD_DIM
    q = q_ref[...] * jnp.asarray(HEAD_DIM ** -0.5, BF16)
    zero = jnp.zeros_like(q)
    qh = [jnp.where(lo_half, q, zero), jnp.where(lo_half, zero, q)]
    acc_ref[...] = jnp.zeros(acc_ref.shape, F32)
    r_ref[...] = jnp.zeros(r_ref.shape, F32)

    def key_tile(j, diag):
        start = pl.multiple_of(j * tile, tile)
        kb = k_ref[pl.ds(start, tile), :]
        vb = v_ref[pl.ds(start, tile), :]
        outs = []
        for hh in range(2):
            z = _dot_nt(qh[hh], kb)
            log_beta = jnp.minimum(z, 0.0) - jnp.log(1.0 + jnp.exp(-jnp.abs(z)))
            log_keep = log_beta - z
            if diag:
                log_keep = jnp.where(strict, log_keep, 0.0)
            hi = log_keep.astype(BF16)
            lo = (log_keep - hi.astype(F32)).astype(BF16)
            within = _dot(jnp.concatenate([hi, lo], axis=1), ucat)
            r = r_ref[hh]
            w = jnp.exp(log_beta + within + jnp.concatenate([r] * (tile // LANES), axis=1))
            if diag:
                w = jnp.where(strict, w, 0.0)
            outs.append(_dot(w.astype(BF16), vb))
            r_ref[hh] = r + jnp.sum(log_keep, axis=-1, keepdims=True)
        acc_ref[...] += jnp.where(lo_half, outs[0], outs[1])

    key_tile(m, True)

    def body(t, carry):
        key_tile(m - 1 - t, False)
        return carry

    lax.fori_loop(0, m, body, 0)
    o_ref[...] = acc_ref[...]


def _sb_attn(sb3, tile):
    b, s, _ = sb3.shape
    npair = SB_HEADS // 2
    return pl.pallas_call(
        functools.partial(_sb_body, tile=tile),
        grid=(b, npair, s // tile),
        in_specs=[pl.BlockSpec((None, tile, LANES), lambda bi, p, i: (bi, i, p)),
                  pl.BlockSpec((None, s, LANES), lambda bi, p, i: (bi, 0, npair + p)),
                  pl.BlockSpec((None, s, LANES), lambda bi, p, i: (bi, 0, 2 * npair + p))],
        out_specs=pl.BlockSpec((None, tile, LANES), lambda bi, p, i: (bi, i, p)),
        out_shape=jax.ShapeDtypeStruct((b, s, SB_WIDTH), F32),
        scratch_shapes=[pltpu.VMEM((tile, LANES), F32), pltpu.VMEM((2, tile, LANES), F32)],
        compiler_params=_params(("parallel", "parallel", "arbitrary")),
        name="sb_attn",
    )(sb3, sb3, sb3)


def _head_rms(x, g):
    lane = lax.broadcasted_iota(jnp.int32, x.shape, 1)
    sq = x * x
    inv = jnp.zeros_like(x)
    for h in range(MEM_HEADS):
        in_head = (lane >= h * HEAD_DIM) & (lane < (h + 1) * HEAD_DIM)
        ms = jnp.sum(jnp.where(in_head, sq, 0.0), axis=-1, keepdims=True) * (1.0 / HEAD_DIM)
        inv = jnp.where(in_head, lax.rsqrt(ms + EPS), inv)
    return x * inv * g


def _memkv_body(m_ref, g_ref, w_ref, kg_ref, k_ref, v_ref):
    h = _rms(m_ref[...], g_ref[...]).astype(BF16)
    kv = _dot(h, w_ref[...])
    k_ref[...] = _head_rms(kv[:, 0:MEM_WIDTH], kg_ref[...]).astype(BF16)
    v_ref[...] = kv[:, MEM_WIDTH:2 * MEM_WIDTH].astype(BF16)


def _mem_kv(mem, g, w, kg_cols):
    b, m, _ = mem.shape
    return pl.pallas_call(
        _memkv_body,
        grid=(b,),
        in_specs=[pl.BlockSpec((None, m, D_MODEL), lambda i: (i, 0, 0)),
                  pl.BlockSpec((1, D_MODEL), lambda i: (0, 0)),
                  pl.BlockSpec((D_MODEL, 2 * MEM_WIDTH), lambda i: (0, 0)),
                  pl.BlockSpec((1, MEM_WIDTH), lambda i: (0, 0))],
        out_specs=[pl.BlockSpec((None, m, MEM_WIDTH), lambda i: (i, 0, 0)),
                   pl.BlockSpec((None, m, MEM_WIDTH), lambda i: (i, 0, 0))],
        out_shape=[jax.ShapeDtypeStruct((b, m, MEM_WIDTH), BF16),
                   jax.ShapeDtypeStruct((b, m, MEM_WIDTH), BF16)],
        compiler_params=_params(("parallel",)),
        name="mem_kv",
    )(mem, g, w, kg_cols)


def _memattn_body(q_ref, k_ref, v_ref, qg_ref, og_ref, o_ref):
    qn = _head_rms(q_ref[...], qg_ref[...]).astype(BF16)
    lane = lax.broadcasted_iota(jnp.int32, qn.shape, 1)
    k = k_ref[...]
    v = v_ref[...]
    out = jnp.zeros(qn.shape, F32)
    for h in range(MEM_HEADS):
        in_head = (lane >= h * HEAD_DIM) & (lane < (h + 1) * HEAD_DIM)
        s = _dot_nt(jnp.where(in_head, qn, jnp.zeros_like(qn)), k) * (HEAD_DIM ** -0.5)
        e = jnp.exp(s - jnp.max(s, axis=-1, keepdims=True))
        p = e / jnp.sum(e, axis=-1, keepdims=True)
        out = jnp.where(in_head, _dot(p.astype(BF16), v), out)
    o_ref[...] = _rms(out, og_ref[...]).astype(BF16)


def _mem_attn(qm3, mk, mv, qg_cols, og, tq):
    b, s, _ = qm3.shape
    m = mk.shape[1]
    return pl.pallas_call(
        _memattn_body,
        grid=(b, s // tq),
        in_specs=[pl.BlockSpec((None, tq, MEM_WIDTH), lambda i, j: (i, j, 0)),
                  pl.BlockSpec((None, m, MEM_WIDTH), lambda i, j: (i, 0, 0)),
                  pl.BlockSpec((None, m, MEM_WIDTH), lambda i, j: (i, 0, 0)),
                  pl.BlockSpec((1, MEM_WIDTH), lambda i, j: (0, 0)),
                  pl.BlockSpec((1, MEM_WIDTH), lambda i, j: (0, 0))],
        out_specs=pl.BlockSpec((None, tq, MEM_WIDTH), lambda i, j: (i, j, 0)),
        out_shape=jax.ShapeDtypeStruct((b, s, MEM_WIDTH), BF16),
        compiler_params=_params(("parallel", "parallel")),
        name="mem_attn",
    )(qm3, mk, mv, qg_cols, og)


def _outproj_body(x_ref, ys_ref, yb_ref, ym_ref, sbg_ref, w_ref, o_ref):
    yb = _rms(yb_ref[...], sbg_ref[...]).astype(BF16)
    acc = _dot(ys_ref[...], w_ref[0:SSD_WIDTH, :])
    acc = acc + _dot(yb, w_ref[SSD_WIDTH:SSD_WIDTH + SB_WIDTH, :])
    acc = acc + _dot(ym_ref[...], w_ref[SSD_WIDTH + SB_WIDTH:D_MODEL, :])
    o_ref[...] = x_ref[...] + acc


def _out_proj(x2, y_ssd, y_sb, y_mem, sb_g, w, tm):
    t = x2.shape[0]
    rows = lambda width: pl.BlockSpec((tm, width), lambda i: (i, 0))
    return pl.pallas_call(
        _outproj_body,
        grid=(t // tm,),
        in_specs=[rows(D_MODEL), rows(SSD_WIDTH), rows(SB_WIDTH), rows(MEM_WIDTH),
                  pl.BlockSpec((1, SB_WIDTH), lambda i: (0, 0)),
                  pl.BlockSpec((D_MODEL, D_MODEL), lambda i: (0, 0))],
        out_specs=rows(D_MODEL),
        out_shape=jax.ShapeDtypeStruct((t, D_MODEL), F32),
        compiler_params=_params(("parallel",), VMEM_LIMIT),
        name="out_proj",
    )(x2, y_ssd, y_sb, y_mem, sb_g, w)


def _top_values(scores, k, want_rank=False):
    work = scores
    vals = []
    rank = jnp.full(scores.shape, 2.0 * k, F32) if want_rank else None
    for r in range(k):
        m = jnp.max(work, axis=0, keepdims=True)
        vals.append(m)
        if want_rank or r + 1 < k:
            hit = work == m
            if want_rank:
                rank = jnp.where(hit, float(r), rank)
            if r + 1 < k:
                work = jnp.where(hit, NEG_INF, work)
    return vals, rank


def _peer_pre_body(x_ref, g_ref, wqt_ref, keys_ref, h2t_ref, ce_ref, rk_ref):
    topk = PEER_TOPK
    half = PEER_DKEY // 2
    h2 = _rms(x_ref[...], g_ref[...])
    h2t = h2.T.astype(BF16)
    h2t_ref[...] = h2t
    qt = _dot(wqt_ref[...], h2t).astype(BF16)
    for h in range(PEER_HEADS):
        s1 = _dot(keys_ref[h, 0], qt[(2 * h) * half:(2 * h + 1) * half, :])
        s2 = _dot(keys_ref[h, 1], qt[(2 * h + 1) * half:(2 * h + 2) * half, :])
        a, _ = _top_values(s1, topk)
        b, rank2 = _top_values(s2, topk, want_rank=True)
        cands = [a[p] + b[q] for p in range(topk) for q in range(topk // (p + 1))]
        pad = (-len(cands)) % 8
        cand = jnp.concatenate(cands + [jnp.full_like(a[0], NEG_INF)] * pad, axis=0)
        tau = _top_values(cand, topk)[0][topk - 1]
        b_sorted = jnp.concatenate(b, axis=0)
        eb_sorted = jnp.exp(b_sorted - b[0])
        zsum = jnp.zeros_like(tau)
        cnt1 = jnp.zeros_like(s1)
        for p in range(topk):
            sel = (a[p] + b_sorted) >= tau
            cnt_p = jnp.sum(jnp.where(sel, 1.0, 0.0), axis=0, keepdims=True)
            zsum = zsum + jnp.exp(a[p] - a[0]) * jnp.sum(jnp.where(sel, eb_sorted, 0.0), axis=0, keepdims=True)
            cnt1 = jnp.where(s1 == a[p], cnt_p, cnt1)
        ce_ref[0, h] = cnt1
        ce_ref[1, h] = jnp.exp(s1 - a[0])
        rk_ref[0, h] = rank2.astype(BF16)
        rk_ref[1, h] = (jnp.exp(s2 - b[0]) * (0.5 / zsum)).astype(BF16)


def _peer_pre(x2, g, wqt, keys, tb):
    t = x2.shape[0]
    info = lambda: pl.BlockSpec((2, PEER_HEADS, PEER_NKEYS, tb), lambda i: (0, 0, 0, i))
    return pl.pallas_call(
        _peer_pre_body,
        grid=(t // tb,),
        in_specs=[pl.BlockSpec((tb, D_MODEL), lambda i: (i, 0)),
                  pl.BlockSpec((1, D_MODEL), lambda i: (0, 0)),
                  pl.BlockSpec((D_MODEL, D_MODEL), lambda i: (0, 0)),
                  pl.BlockSpec((PEER_HEADS, 2, PEER_NKEYS, PEER_DKEY // 2), lambda i: (0, 0, 0, 0))],
        out_specs=[pl.BlockSpec((D_MODEL, tb), lambda i: (0, i)), info(), info()],
        out_shape=[jax.ShapeDtypeStruct((D_MODEL, t), BF16),
                   jax.ShapeDtypeStruct((2, PEER_HEADS, PEER_NKEYS, t), F32),
                   jax.ShapeDtypeStruct((2, PEER_HEADS, PEER_NKEYS, t), BF16)],
        compiler_params=_params(("parallel",), VMEM_LIMIT),
        name="peer_pre",
    )(x2, g, wqt, keys)


def _peer_main_body(h2t_ref, ce_ref, rk_ref, u_ref, vt_ref, x_ref, o_ref, acc_ref, pt_ref, *, eb):
    e = pl.program_id(1)
    sub = eb // PEER_NKEYS

    @pl.when(e == 0)
    def _():
        acc_ref[...] = jnp.zeros(acc_ref.shape, F32)

    at = _dot(u_ref[...], h2t_ref[...])
    act = (at * (1.0 + lax.erf(at * math.sqrt(0.5)))).astype(BF16)
    tile = (PEER_NKEYS, at.shape[1])
    zero = jnp.zeros(tile, BF16)
    for r in range(sub):
        i = e * sub + r
        wd = None
        for h in range(PEER_HEADS):
            cnt = jnp.broadcast_to(ce_ref[0, h, pl.ds(i, 1), :].astype(BF16), tile)
            ea = jnp.broadcast_to(ce_ref[1, h, pl.ds(i, 1), :].astype(BF16), tile)
            term = jnp.where(rk_ref[0, h] < cnt, rk_ref[1, h], zero) * ea
            wd = term if wd is None else wd + term
        pt_ref[r * PEER_NKEYS:(r + 1) * PEER_NKEYS, :] = wd * act[r * PEER_NKEYS:(r + 1) * PEER_NKEYS, :]
    acc_ref[...] += _dot(vt_ref[...], pt_ref[...])

    @pl.when(e == pl.num_programs(1) - 1)
    def _():
        o_ref[...] = x_ref[...] + acc_ref[...].T


def _peer_main(x2, h2t, ce, rk, u_b, vt_b, tb, eb):
    t = x2.shape[0]
    info = lambda: pl.BlockSpec((2, PEER_HEADS, PEER_NKEYS, tb), lambda i, e: (0, 0, 0, i))
    return pl.pallas_call(
        functools.partial(_peer_main_body, eb=eb),
        grid=(t // tb, PEER_EXPERTS // eb),
        in_specs=[pl.BlockSpec((D_MODEL, tb), lambda i, e: (0, i)), info(), info(),
                  pl.BlockSpec((eb, D_MODEL), lambda i, e: (e, 0)),
                  pl.BlockSpec((D_MODEL, eb), lambda i, e: (0, e)),
                  pl.BlockSpec((tb, D_MODEL), lambda i, e: (i, 0))],
        out_specs=pl.BlockSpec((tb, D_MODEL), lambda i, e: (i, 0)),
        out_shape=jax.ShapeDtypeStruct((t, D_MODEL), F32),
        scratch_shapes=[pltpu.VMEM((D_MODEL, tb), F32), pltpu.VMEM((eb, tb), BF16)],
        compiler_params=_params(("parallel", "arbitrary"), VMEM_LIMIT),
        name="peer_main",
    )(h2t, ce, rk, u_b, vt_b, x2)


def _pick(n, pref):
    return pref if n % pref == 0 else n


def _permute_w_in(w):
    pts = [0]
    for width in IN_SPLITS:
        pts.append(pts[-1] + width)
    z, xbc, dtw, q, k, v, qm = [w[:, pts[n]:pts[n + 1]] for n in range(len(IN_SPLITS))]
    dtw = jnp.pad(dtw, ((0, 0), (0, DT_PAD - SSD_HEADS)))
    return jnp.concatenate([z, xbc, dtw, q, k, v, qm], axis=1).astype(BF16)


def _pad_row(v, width):
    return jnp.pad(v, (0, width - v.shape[0])).reshape(1, width)


def kernel(x, mem, norm1_g, w_in, conv_w, conv_b, dt_bias, a_log, d_skip, ssd_norm_g, sb_out_g, mem_norm_g,
           w_mem_kv, mq_norm_g, mk_norm_g, mem_out_g, w_out, norm2_g, peer_wq, peer_keys, peer_u, peer_v):
    bsz, seqlen, _ = x.shape
    t = bsz * seqlen
    depth = w_in.shape[0]
    tm = _pick(t, 512)
    tq = _pick(seqlen, 512)
    tb = _pick(t, 512)
    eb = 1024
    row = lambda v: v.reshape(1, -1)
    x2 = x.reshape(t, D_MODEL)
    for l in range(depth):
        u_ssd, u_sb, u_qm = _in_proj(x2, row(norm1_g[l]), _permute_w_in(w_in[l]), tm)
        y_ssd = _ssd(u_ssd.reshape(bsz, seqlen, SSD_OUT), conv_w[l], row(conv_b[l]),
                     _pad_row(dt_bias[l], DT_PAD), _pad_row(a_log[l], DT_PAD),
                     row(jnp.repeat(d_skip[l], HEAD_DIM)), row(ssd_norm_g[l]))
        y_sb = _sb_attn(u_sb.reshape(bsz, seqlen, SB_OUT), _pick(seqlen, 256))
        mk, mv = _mem_kv(mem, row(mem_norm_g[l]), w_mem_kv[l].astype(BF16),
                         row(jnp.tile(mk_norm_g[l], MEM_HEADS)))
        y_mem = _mem_attn(u_qm.reshape(bsz, seqlen, MEM_WIDTH), mk, mv,
                          row(jnp.tile(mq_norm_g[l], MEM_HEADS)), row(mem_out_g[l]), tq)
        x2 = _out_proj(x2, y_ssd.reshape(t, SSD_WIDTH), y_sb.reshape(t, SB_WIDTH),
                       y_mem.reshape(t, MEM_WIDTH), row(sb_out_g[l]), w_out[l].astype(BF16), tm)
        h2t, ce, rk = _peer_pre(x2, row(norm2_g[l]), peer_wq[l].T.astype(BF16), peer_keys[l].astype(BF16), tb)
        x2 = _peer_main(x2, h2t, ce, rk, peer_u[l].astype(BF16), peer_v[l].T.astype(BF16), tb, eb)
    return x2.reshape(bsz, seqlen, D_MODEL)
```

```python
import functools
import math

import jax
import jax.numpy as jnp
from jax import lax
from jax.experimental import pallas as pl
from jax.experimental.pallas import tpu as pltpu

F32 = jnp.float32
BF16 = jnp.bfloat16

D_MODEL = 1024
HEAD_DIM = 64
SSD_HEADS = 6
SSD_WIDTH = SSD_HEADS * HEAD_DIM
SSD_GROUPS = 2
SSD_STATE = 64
SSD_CONV = 4
SSD_CHUNK = 128
SSD_CONV_CH = SSD_WIDTH + 2 * SSD_GROUPS * SSD_STATE
SB_HEADS = 6
SB_WIDTH = SB_HEADS * HEAD_DIM
SB_BLOCK = 128
MEM_HEADS = 4
MEM_WIDTH = MEM_HEADS * HEAD_DIM
IN_SPLITS = [SSD_WIDTH, SSD_CONV_CH, SSD_HEADS, SB_WIDTH, SB_WIDTH, SB_WIDTH, MEM_WIDTH]
PEER_HEADS = 8
PEER_NKEYS = 128
PEER_EXPERTS = PEER_NKEYS * PEER_NKEYS
PEER_DKEY = 128
PEER_TOPK = 16
EPS = 1e-6

LANES = 128
DT_PAD = LANES
SSD_OUT = SSD_WIDTH + SSD_CONV_CH + DT_PAD
SB_OUT = 3 * SB_WIDTH
IN_COLS = SSD_OUT + SB_OUT + MEM_WIDTH
VMEM_LIMIT = 48 * 1024 * 1024

NEG_INF = float("-inf")


def _params(sem, vmem=None, flags=None):
    return pltpu.CompilerParams(dimension_semantics=sem, vmem_limit_bytes=vmem, flags=flags)


def _rms(x, g):
    ms = jnp.mean(x * x, axis=-1, keepdims=True)
    return x * lax.rsqrt(ms + EPS) * g


def _sigmoid(x):
    return 1.0 / (1.0 + jnp.exp(-x))


def _dot(a, b):
    return jnp.dot(a, b, preferred_element_type=F32)


def _dot_nt(a, b):
    return lax.dot_general(a, b, (((1,), (1,)), ((), ())), preferred_element_type=F32)


def _split3(x):
    hi = x.astype(BF16)
    r1 = x - hi.astype(F32)
    mid = r1.astype(BF16)
    lo = (r1 - mid.astype(F32)).astype(BF16)
    return hi, mid, lo


def _inproj_body(x_ref, g_ref, w_ref, ssd_ref, sb_ref, qm_ref):
    h = _rms(x_ref[...], g_ref[...]).astype(BF16)
    ssd_ref[...] = _dot(h, w_ref[:, 0:SSD_OUT])
    sb_ref[...] = _dot(h, w_ref[:, SSD_OUT:SSD_OUT + SB_OUT]).astype(BF16)
    qm_ref[...] = _dot(h, w_ref[:, SSD_OUT + SB_OUT:IN_COLS])


def _in_proj(x2, g, w, tm):
    t = x2.shape[0]
    return pl.pallas_call(
        _inproj_body,
        grid=(t // tm,),
        in_specs=[pl.BlockSpec((tm, D_MODEL), lambda i: (i, 0)),
                  pl.BlockSpec((1, D_MODEL), lambda i: (0, 0)),
                  pl.BlockSpec((D_MODEL, IN_COLS), lambda i: (0, 0))],
        out_specs=[pl.BlockSpec((tm, SSD_OUT), lambda i: (i, 0)),
                   pl.BlockSpec((tm, SB_OUT), lambda i: (i, 0)),
                   pl.BlockSpec((tm, MEM_WIDTH), lambda i: (i, 0))],
        out_shape=[jax.ShapeDtypeStruct((t, SSD_OUT), F32),
                   jax.ShapeDtypeStruct((t, SB_OUT), BF16),
                   jax.ShapeDtypeStruct((t, MEM_WIDTH), F32)],
        compiler_params=_params(("parallel",), VMEM_LIMIT),
        name="in_proj",
    )(x2, g, w)


def _ssd_body(u_ref, cw_ref, cb_ref, dtb_ref, alog_ref, dsk_ref, ng_ref, y_ref, xpad_ref, st_ref):
    q_len = SSD_CHUNK
    c = pl.program_id(1)

    @pl.when(c == 0)
    def _():
        xpad_ref[0:8, :] = jnp.zeros((8, SSD_CONV_CH), F32)
        st_ref[...] = jnp.zeros(st_ref.shape, F32)

    xpad_ref[8:8 + q_len, :] = u_ref[:, SSD_WIDTH:SSD_WIDTH + SSD_CONV_CH]
    conv = cb_ref[...]
    for k in range(SSD_CONV):
        off = 8 - (SSD_CONV - 1) + k
        conv = conv + cw_ref[k:k + 1, :] * xpad_ref[off:off + q_len, :]
    xpad_ref[0:8, :] = xpad_ref[q_len:q_len + 8, :]
    xbc = conv * _sigmoid(conv)

    xs = xbc[:, 0:SSD_WIDTH]
    bm = xbc[:, SSD_WIDTH:SSD_WIDTH + LANES]
    cm = xbc[:, SSD_WIDTH + LANES:SSD_WIDTH + 2 * LANES]

    udt = u_ref[:, SSD_WIDTH + SSD_CONV_CH:SSD_OUT] + dtb_ref[...]
    dt = jnp.maximum(udt, 0.0) + jnp.log1p(jnp.exp(-jnp.abs(udt)))
    adt = dt * (-jnp.exp(alog_ref[...]))

    row = lax.broadcasted_iota(jnp.int32, (q_len, q_len), 0)
    lane = lax.broadcasted_iota(jnp.int32, (q_len, q_len), 1)
    causal = row >= lane
    lo_half = lane < HEAD_DIM
    tri = jnp.where(causal, 1.0, 0.0).astype(BF16)
    hi, mid, lo = _split3(adt)
    acs = _dot(tri, hi) + _dot(tri, mid) + _dot(tri, lo)
    acs_t = acs.T
    last = acs[q_len - 1:q_len, :]
    dec_s = jnp.exp(last - acs)
    eacs = jnp.exp(acs)
    cdec = jnp.exp(last)

    bm_b = bm.astype(BF16)
    bt_b = bm.T.astype(BF16)
    cmask = [jnp.where(lo_half, cm, 0.0).astype(BF16), jnp.where(lo_half, 0.0, cm).astype(BF16)]
    cb = [_dot_nt(cmask[g], bm_b) for g in range(SSD_GROUPS)]

    def col_pair(m, h0, h1):
        return jnp.where(lo_half, m[:, h0:h0 + 1], m[:, h1:h1 + 1])

    def lmat(h):
        seg = acs[:, h:h + 1] - acs_t[h:h + 1, :]
        return jnp.exp(jnp.where(causal, seg, NEG_INF))

    heads_per_group = SSD_HEADS // SSD_GROUPS
    ys = []
    for p in range(SSD_HEADS // 2):
        h0, h1 = 2 * p, 2 * p + 1
        g0, g1 = h0 // heads_per_group, h1 // heads_per_group
        xp = xs[:, p * LANES:(p + 1) * LANES]
        xdt = xp * col_pair(dt, h0, h1)
        xdt_b = xdt.astype(BF16)
        m0 = (cb[g0] * lmat(h0)).astype(BF16)
        m1 = (cb[g1] * lmat(h1)).astype(BF16)
        y_diag = jnp.where(lo_half, _dot(m0, xdt_b), _dot(m1, xdt_b))
        st = st_ref[p]
        st_b = st.astype(BF16)
        if g0 == g1:
            y_off = _dot(cmask[g0], st_b)
        else:
            y_off = jnp.where(lo_half, _dot(cmask[g0], st_b), _dot(cmask[g1], st_b))
        y_off = y_off * col_pair(eacs, h0, h1)
        dx = (xdt * col_pair(dec_s, h0, h1)).astype(BF16)
        st_ref[p] = st * col_pair(cdec, h0, h1) + _dot(bt_b, dx)
        ys.append(y_diag + y_off + xp * dsk_ref[:, p * LANES:(p + 1) * LANES])
    y = jnp.concatenate(ys, axis=1)
    z = u_ref[:, 0:SSD_WIDTH]
    y_ref[...] = _rms(y * (z * _sigmoid(z)), ng_ref[...]).astype(BF16)


def _ssd(u3, conv_w, conv_b, dt_bias, a_log, d_skip_cols, norm_g):
    b, s, _ = u3.shape
    nc = s // SSD_CHUNK
    const = lambda shape: pl.BlockSpec(shape, lambda i, j: (0,) * len(shape))
    return pl.pallas_call(
        _ssd_body,
        grid=(b, nc),
        in_specs=[pl.BlockSpec((None, SSD_CHUNK, SSD_OUT), lambda i, j: (i, j, 0)),
                  const((SSD_CONV, SSD_CONV_CH)), const((1, SSD_CONV_CH)),
                  const((1, DT_PAD)), const((1, DT_PAD)),
                  const((1, SSD_WIDTH)), const((1, SSD_WIDTH))],
        out_specs=pl.BlockSpec((None, SSD_CHUNK, SSD_WIDTH), lambda i, j: (i, j, 0)),
        out_shape=jax.ShapeDtypeStruct((b, s, SSD_WIDTH), BF16),
        scratch_shapes=[pltpu.VMEM((SSD_CHUNK + 8, SSD_CONV_CH), F32),
                        pltpu.VMEM((SSD_HEADS // 2, LANES, LANES), F32)],
        compiler_params=_params(("parallel", "arbitrary")),
        name="ssd",
    )(u3, conv_w, conv_b, dt_bias, a_log, d_skip_cols, norm_g)


def _sb_body(q_ref, k_ref, v_ref, g_ref, o_ref, acc_ref, r_ref, z_ref, lb_ref, cs_ref, rs_ref, *, tile):
    m = pl.program_id(1)
    npair = SB_HEADS // 2
    row = lax.broadcasted_iota(jnp.int32, (tile, tile), 0)
    col = lax.broadcasted_iota(jnp.int32, (tile, tile), 1)
    strict = col < row
    upper = jnp.where(row > col, 1.0, 0.0).astype(BF16)
    ucat = jnp.concatenate([upper, upper], axis=0)
    lo_half = lax.broadcasted_iota(jnp.int32, (tile, LANES), 1) < HEAD_DIM
    qh = []
    for p in range(npair):
        q = q_ref[:, p * LANES:(p + 1) * LANES] * jnp.asarray(HEAD_DIM ** -0.5, BF16)
        zero = jnp.zeros_like(q)
        qh += [jnp.where(lo_half, q, zero), jnp.where(lo_half, zero, q)]
    acc_ref[...] = jnp.zeros(acc_ref.shape, F32)
    r_ref[...] = jnp.zeros(r_ref.shape, F32)

    def key_tile(j, diag):
        start = pl.multiple_of(j * tile, tile)

        def scores(h):
            kb = k_ref[pl.ds(start, tile), (h // 2) * LANES:(h // 2 + 1) * LANES]
            z_ref[h % 2] = _dot_nt(qh[h], kb)

        def log_terms(h):
            z = z_ref[h % 2]
            log_beta = jnp.minimum(z, 0.0) - jnp.log(1.0 + jnp.exp(-jnp.abs(z)))
            log_keep = log_beta - z
            if diag:
                log_keep = jnp.where(strict, log_keep, 0.0)
            hi = log_keep.astype(BF16)
            lo = (log_keep - hi.astype(F32)).astype(BF16)
            lb_ref[h % 2] = log_beta
            rs_ref[h % 2] = jnp.broadcast_to(jnp.sum(log_keep, axis=-1, keepdims=True), (tile, LANES))
            cs_ref[h % 2] = _dot(jnp.concatenate([hi, lo], axis=1), ucat)

        def weights(h):
            r = r_ref[h]
            w = jnp.exp(lb_ref[h % 2] + cs_ref[h % 2] + jnp.concatenate([r] * (tile // LANES), axis=1))
            if diag:
                w = jnp.where(strict, w, 0.0)
            r_ref[h] = r + rs_ref[h % 2]
            vb = v_ref[pl.ds(start, tile), (h // 2) * LANES:(h // 2 + 1) * LANES]
            out = _dot(w.astype(BF16), vb)
            keep = lo_half if h % 2 == 0 else jnp.logical_not(lo_half)
            acc_ref[h // 2] += jnp.where(keep, out, 0.0)

        scores(0)
        for h in range(SB_HEADS):
            if h + 1 < SB_HEADS:
                scores(h + 1)
            log_terms(h)
            if h >= 1:
                weights(h - 1)
        weights(SB_HEADS - 1)

    key_tile(m, True)

    def body(t, carry):
        key_tile(m - 1 - t, False)
        return carry

    lax.fori_loop(0, m, body, 0)
    y = jnp.concatenate([acc_ref[p] for p in range(npair)], axis=1)
    o_ref[...] = _rms(y, g_ref[...]).astype(BF16)


def _sb_attn(sb3, out_g, tile):
    b, s, _ = sb3.shape
    return pl.pallas_call(
        functools.partial(_sb_body, tile=tile),
        grid=(b, s // tile),
        in_specs=[pl.BlockSpec((None, tile, SB_WIDTH), lambda bi, i: (bi, i, 0)),
                  pl.BlockSpec((None, s, SB_WIDTH), lambda bi, i: (bi, 0, 1)),
                  pl.BlockSpec((None, s, SB_WIDTH), lambda bi, i: (bi, 0, 2)),
                  pl.BlockSpec((1, SB_WIDTH), lambda bi, i: (0, 0))],
        out_specs=pl.BlockSpec((None, tile, SB_WIDTH), lambda bi, i: (bi, i, 0)),
        out_shape=jax.ShapeDtypeStruct((b, s, SB_WIDTH), BF16),
        scratch_shapes=[pltpu.VMEM((SB_HEADS // 2, tile, LANES), F32),
                        pltpu.VMEM((SB_HEADS, tile, LANES), F32),
                        pltpu.VMEM((2, tile, tile), F32), pltpu.VMEM((2, tile, tile), F32),
                        pltpu.VMEM((2, tile, tile), F32), pltpu.VMEM((2, tile, LANES), F32)],
        compiler_params=_params(("parallel", "arbitrary"), VMEM_LIMIT),
        name="sb_attn",
    )(sb3, sb3, sb3, out_g)


def _head_rms(x, g):
    lane = lax.broadcasted_iota(jnp.int32, x.shape, 1)
    sq = x * x
    inv = jnp.zeros_like(x)
    for h in range(MEM_HEADS):
        in_head = (lane >= h * HEAD_DIM) & (lane < (h + 1) * HEAD_DIM)
        ms = jnp.sum(jnp.where(in_head, sq, 0.0), axis=-1, keepdims=True) * (1.0 / HEAD_DIM)
        inv = jnp.where(in_head, lax.rsqrt(ms + EPS), inv)
    return x * inv * g


def _memkv_body(m_ref, g_ref, w_ref, kg_ref, k_ref, v_ref):
    h = _rms(m_ref[...], g_ref[...]).astype(BF16)
    kv = _dot(h, w_ref[...])
    k_ref[...] = _head_rms(kv[:, 0:MEM_WIDTH], kg_ref[...]).astype(BF16)
    v_ref[...] = kv[:, MEM_WIDTH:2 * MEM_WIDTH].astype(BF16)


def _mem_kv(mem, g, w, kg_cols):
    b, m, _ = mem.shape
    return pl.pallas_call(
        _memkv_body,
        grid=(b,),
        in_specs=[pl.BlockSpec((None, m, D_MODEL), lambda i: (i, 0, 0)),
                  pl.BlockSpec((1, D_MODEL), lambda i: (0, 0)),
                  pl.BlockSpec((D_MODEL, 2 * MEM_WIDTH), lambda i: (0, 0)),
                  pl.BlockSpec((1, MEM_WIDTH), lambda i: (0, 0))],
        out_specs=[pl.BlockSpec((None, m, MEM_WIDTH), lambda i: (i, 0, 0)),
                   pl.BlockSpec((None, m, MEM_WIDTH), lambda i: (i, 0, 0))],
        out_shape=[jax.ShapeDtypeStruct((b, m, MEM_WIDTH), BF16),
                   jax.ShapeDtypeStruct((b, m, MEM_WIDTH), BF16)],
        compiler_params=_params(("parallel",)),
        name="mem_kv",
    )(mem, g, w, kg_cols)


def _memattn_body(q_ref, k_ref, v_ref, qg_ref, og_ref, o_ref):
    qn = _head_rms(q_ref[...], qg_ref[...]).astype(BF16)
    lane = lax.broadcasted_iota(jnp.int32, qn.shape, 1)
    k = k_ref[...]
    v = v_ref[...]
    out = jnp.zeros(qn.shape, F32)
    for h in range(MEM_HEADS):
        in_head = (lane >= h * HEAD_DIM) & (lane < (h + 1) * HEAD_DIM)
        s = _dot_nt(jnp.where(in_head, qn, jnp.zeros_like(qn)), k) * (HEAD_DIM ** -0.5)
        e = jnp.exp(s - jnp.max(s, axis=-1, keepdims=True))
        p = e / jnp.sum(e, axis=-1, keepdims=True)
        out = jnp.where(in_head, _dot(p.astype(BF16), v), out)
    o_ref[...] = _rms(out, og_ref[...]).astype(BF16)


def _mem_attn(qm3, mk, mv, qg_cols, og, tq):
    b, s, _ = qm3.shape
    m = mk.shape[1]
    return pl.pallas_call(
        _memattn_body,
        grid=(b, s // tq),
        in_specs=[pl.BlockSpec((None, tq, MEM_WIDTH), lambda i, j: (i, j, 0)),
                  pl.BlockSpec((None, m, MEM_WIDTH), lambda i, j: (i, 0, 0)),
                  pl.BlockSpec((None, m, MEM_WIDTH), lambda i, j: (i, 0, 0)),
                  pl.BlockSpec((1, MEM_WIDTH), lambda i, j: (0, 0)),
                  pl.BlockSpec((1, MEM_WIDTH), lambda i, j: (0, 0))],
        out_specs=pl.BlockSpec((None, tq, MEM_WIDTH), lambda i, j: (i, j, 0)),
        out_shape=jax.ShapeDtypeStruct((b, s, MEM_WIDTH), BF16),
        compiler_params=_params(("parallel", "parallel")),
        name="mem_attn",
    )(qm3, mk, mv, qg_cols, og)


def _outproj_body(x_ref, ys_ref, yb_ref, ym_ref, w_ref, o_ref):
    acc = _dot(ys_ref[...], w_ref[0:SSD_WIDTH, :])
    acc = acc + _dot(yb_ref[...], w_ref[SSD_WIDTH:SSD_WIDTH + SB_WIDTH, :])
    acc = acc + _dot(ym_ref[...], w_ref[SSD_WIDTH + SB_WIDTH:D_MODEL, :])
    o_ref[...] = x_ref[...] + acc


def _out_proj(x2, y_ssd, y_sb, y_mem, w, tm):
    t = x2.shape[0]
    rows = lambda width: pl.BlockSpec((tm, width), lambda i: (i, 0))
    return pl.pallas_call(
        _outproj_body,
        grid=(t // tm,),
        in_specs=[rows(D_MODEL), rows(SSD_WIDTH), rows(SB_WIDTH), rows(MEM_WIDTH),
                  pl.BlockSpec((D_MODEL, D_MODEL), lambda i: (0, 0))],
        out_specs=rows(D_MODEL),
        out_shape=jax.ShapeDtypeStruct((t, D_MODEL), F32),
        compiler_params=_params(("parallel",), VMEM_LIMIT),
        name="out_proj",
    )(x2, y_ssd, y_sb, y_mem, w)


def _top_values(scores, k, want_rank=False):
    work = scores
    vals = []
    rank = jnp.full(scores.shape, 2.0 * k, F32) if want_rank else None
    for r in range(k):
        m = jnp.max(work, axis=0, keepdims=True)
        vals.append(m)
        if want_rank or r + 1 < k:
            hit = work == m
            if want_rank:
                rank = jnp.where(hit, float(r), rank)
            if r + 1 < k:
                work = jnp.where(hit, NEG_INF, work)
    return vals, rank


def _peer_pre_body(x_ref, g_ref, wqt_ref, keys_ref, h2t_ref, ce_ref, rk_ref):
    topk = PEER_TOPK
    half = PEER_DKEY // 2
    h2 = _rms(x_ref[...], g_ref[...])
    h2t = h2.T.astype(BF16)
    h2t_ref[...] = h2t
    qt = _dot(wqt_ref[...], h2t).astype(BF16)
    for h in range(PEER_HEADS):
        s1 = _dot(keys_ref[h, 0], qt[(2 * h) * half:(2 * h + 1) * half, :])
        s2 = _dot(keys_ref[h, 1], qt[(2 * h + 1) * half:(2 * h + 2) * half, :])
        a, _ = _top_values(s1, topk)
        b, rank2 = _top_values(s2, topk, want_rank=True)
        cands = [a[p] + b[q] for p in range(topk) for q in range(topk // (p + 1))]
        pad = (-len(cands)) % 8
        cand = jnp.concatenate(cands + [jnp.full_like(a[0], NEG_INF)] * pad, axis=0)
        tau = _top_values(cand, topk)[0][topk - 1]
        b_sorted = jnp.concatenate(b, axis=0)
        eb_sorted = jnp.exp(b_sorted - b[0])
        zsum = jnp.zeros_like(tau)
        cnt1 = jnp.zeros_like(s1)
        for p in range(topk):
            sel = (a[p] + b_sorted) >= tau
            cnt_p = jnp.sum(jnp.where(sel, 1.0, 0.0), axis=0, keepdims=True)
            zsum = zsum + jnp.exp(a[p] - a[0]) * jnp.sum(jnp.where(sel, eb_sorted, 0.0), axis=0, keepdims=True)
            cnt1 = jnp.where(s1 == a[p], cnt_p, cnt1)
        ce_ref[0, h] = cnt1
        ce_ref[1, h] = jnp.exp(s1 - a[0])
        rk_ref[0, h] = rank2.astype(BF16)
        rk_ref[1, h] = (jnp.exp(s2 - b[0]) * (0.5 / zsum)).astype(BF16)


def _peer_pre(x2, g, wqt, keys, tb):
    t = x2.shape[0]
    info = lambda: pl.BlockSpec((2, PEER_HEADS, PEER_NKEYS, tb), lambda i: (0, 0, 0, i))
    return pl.pallas_call(
        _peer_pre_body,
        grid=(t // tb,),
        in_specs=[pl.BlockSpec((tb, D_MODEL), lambda i: (i, 0)),
                  pl.BlockSpec((1, D_MODEL), lambda i: (0, 0)),
                  pl.BlockSpec((D_MODEL, D_MODEL), lambda i: (0, 0)),
                  pl.BlockSpec((PEER_HEADS, 2, PEER_NKEYS, PEER_DKEY // 2), lambda i: (0, 0, 0, 0))],
        out_specs=[pl.BlockSpec((D_MODEL, tb), lambda i: (0, i)), info(), info()],
        out_shape=[jax.ShapeDtypeStruct((D_MODEL, t), BF16),
                   jax.ShapeDtypeStruct((2, PEER_HEADS, PEER_NKEYS, t), F32),
                   jax.ShapeDtypeStruct((2, PEER_HEADS, PEER_NKEYS, t), BF16)],
        compiler_params=_params(("parallel",), VMEM_LIMIT),
        name="peer_pre",
    )(x2, g, wqt, keys)


def _peer_main_body(h2t_ref, ce_ref, rk_ref, u_ref, vt_ref, x_ref, o_ref, acc_ref, pt_ref, *, eb):
    e = pl.program_id(1)
    sub = eb // PEER_NKEYS

    @pl.when(e == 0)
    def _():
        acc_ref[...] = jnp.zeros(acc_ref.shape, F32)

    at = _dot(u_ref[...], h2t_ref[...])
    act = (at * (1.0 + lax.erf(at * math.sqrt(0.5)))).astype(BF16)
    tile = (PEER_NKEYS, at.shape[1])
    zero = jnp.zeros(tile, BF16)
    for r in range(sub):
        i = e * sub + r
        wd = None
        for h in range(PEER_HEADS):
            cnt = jnp.broadcast_to(ce_ref[0, h, pl.ds(i, 1), :].astype(BF16), tile)
            ea = jnp.broadcast_to(ce_ref[1, h, pl.ds(i, 1), :].astype(BF16), tile)
            term = jnp.where(rk_ref[0, h] < cnt, rk_ref[1, h], zero) * ea
            wd = term if wd is None else wd + term
        pt_ref[r * PEER_NKEYS:(r + 1) * PEER_NKEYS, :] = wd * act[r * PEER_NKEYS:(r + 1) * PEER_NKEYS, :]
    acc_ref[...] += _dot(vt_ref[...], pt_ref[...])

    @pl.when(e == pl.num_programs(1) - 1)
    def _():
        o_ref[...] = x_ref[...] + acc_ref[...].T


def _peer_main(x2, h2t, ce, rk, u_b, vt_b, tb, eb):
    t = x2.shape[0]
    info = lambda: pl.BlockSpec((2, PEER_HEADS, PEER_NKEYS, tb), lambda i, e: (0, 0, 0, i))
    return pl.pallas_call(
        functools.partial(_peer_main_body, eb=eb),
        grid=(t // tb, PEER_EXPERTS // eb),
        in_specs=[pl.BlockSpec((D_MODEL, tb), lambda i, e: (0, i)), info(), info(),
                  pl.BlockSpec((eb, D_MODEL), lambda i, e: (e, 0)),
                  pl.BlockSpec((D_MODEL, eb), lambda i, e: (0, e)),
                  pl.BlockSpec((tb, D_MODEL), lambda i, e: (i, 0))],
        out_specs=pl.BlockSpec((tb, D_MODEL), lambda i, e: (i, 0)),
        out_shape=jax.ShapeDtypeStruct((t, D_MODEL), F32),
        scratch_shapes=[pltpu.VMEM((D_MODEL, tb), F32), pltpu.VMEM((eb, tb), BF16)],
        compiler_params=_params(("parallel", "arbitrary"), VMEM_LIMIT),
        name="peer_main",
    )(h2t, ce, rk, u_b, vt_b, x2)


def _pick(n, pref):
    return pref if n % pref == 0 else n


def _permute_w_in(w):
    pts = [0]
    for width in IN_SPLITS:
        pts.append(pts[-1] + width)
    z, xbc, dtw, q, k, v, qm = [w[:, pts[n]:pts[n + 1]] for n in range(len(IN_SPLITS))]
    dtw = jnp.pad(dtw, ((0, 0), (0, DT_PAD - SSD_HEADS)))
    return jnp.concatenate([z, xbc, dtw, q, k, v, qm], axis=1).astype(BF16)


def _pad_row(v, width):
    return jnp.pad(v, (0, width - v.shape[0])).reshape(1, width)


def kernel(x, mem, norm1_g, w_in, conv_w, conv_b, dt_bias, a_log, d_skip, ssd_norm_g, sb_out_g, mem_norm_g,
           w_mem_kv, mq_norm_g, mk_norm_g, mem_out_g, w_out, norm2_g, peer_wq, peer_keys, peer_u, peer_v):
    bsz, seqlen, _ = x.shape
    t = bsz * seqlen
    depth = w_in.shape[0]
    tm = _pick(t, 512)
    tq = _pick(seqlen, 512)
    tb = _pick(t, 512)
    eb = 1024
    row = lambda v: v.reshape(1, -1)
    x2 = x.reshape(t, D_MODEL)
    for l in range(depth):
        u_ssd, u_sb, u_qm = _in_proj(x2, row(norm1_g[l]), _permute_w_in(w_in[l]), tm)
        y_ssd = _ssd(u_ssd.reshape(bsz, seqlen, SSD_OUT), conv_w[l], row(conv_b[l]),
                     _pad_row(dt_bias[l], DT_PAD), _pad_row(a_log[l], DT_PAD),
                     row(jnp.repeat(d_skip[l], HEAD_DIM)), row(ssd_norm_g[l]))
        y_sb = _sb_attn(u_sb.reshape(bsz, seqlen, SB_OUT), row(sb_out_g[l]), _pick(seqlen, 256))
        mk, mv = _mem_kv(mem, row(mem_norm_g[l]), w_mem_kv[l].astype(BF16),
                         row(jnp.tile(mk_norm_g[l], MEM_HEADS)))
        y_mem = _mem_attn(u_qm.reshape(bsz, seqlen, MEM_WIDTH), mk, mv,
                          row(jnp.tile(mq_norm_g[l], MEM_HEADS)), row(mem_out_g[l]), tq)
        x2 = _out_proj(x2, y_ssd.reshape(t, SSD_WIDTH), y_sb.reshape(t, SB_WIDTH),
                       y_mem.reshape(t, MEM_WIDTH), w_out[l].astype(BF16), tm)
        h2t, ce, rk = _peer_pre(x2, row(norm2_g[l]), peer_wq[l].T.astype(BF16), peer_keys[l].astype(BF16), tb)
        x2 = _peer_main(x2, h2t, ce, rk, peer_u[l].astype(BF16), peer_v[l].T.astype(BF16), tb, eb)
    return x2.reshape(bsz, seqlen, D_MODEL)
```

```python
import functools
import math

import jax
import jax.numpy as jnp
from jax import lax
from jax.experimental import pallas as pl
from jax.experimental.pallas import tpu as pltpu

F32 = jnp.float32
BF16 = jnp.bfloat16

D_MODEL = 1024
HEAD_DIM = 64
SSD_HEADS = 6
SSD_WIDTH = SSD_HEADS * HEAD_DIM
SSD_GROUPS = 2
SSD_STATE = 64
SSD_CONV = 4
SSD_CHUNK = 128
SSD_CONV_CH = SSD_WIDTH + 2 * SSD_GROUPS * SSD_STATE
SB_HEADS = 6
SB_WIDTH = SB_HEADS * HEAD_DIM
SB_BLOCK = 128
MEM_HEADS = 4
MEM_WIDTH = MEM_HEADS * HEAD_DIM
IN_SPLITS = [SSD_WIDTH, SSD_CONV_CH, SSD_HEADS, SB_WIDTH, SB_WIDTH, SB_WIDTH, MEM_WIDTH]
PEER_HEADS = 8
PEER_NKEYS = 128
PEER_EXPERTS = PEER_NKEYS * PEER_NKEYS
PEER_DKEY = 128
PEER_TOPK = 16
EPS = 1e-6

LANES = 128
DT_PAD = LANES
SSD_OUT = SSD_WIDTH + SSD_CONV_CH + DT_PAD
SB_OUT = 3 * SB_WIDTH
IN_COLS = SSD_OUT + SB_OUT + MEM_WIDTH
VMEM_LIMIT = 48 * 1024 * 1024

NEG_INF = float("-inf")


def _params(sem, vmem=None, flags=None):
    return pltpu.CompilerParams(dimension_semantics=sem, vmem_limit_bytes=vmem, flags=flags)


def _rms(x, g):
    ms = jnp.mean(x * x, axis=-1, keepdims=True)
    return x * lax.rsqrt(ms + EPS) * g


def _sigmoid(x):
    return 1.0 / (1.0 + jnp.exp(-x))


def _dot(a, b):
    return jnp.dot(a, b, preferred_element_type=F32)


def _dot_nt(a, b):
    return lax.dot_general(a, b, (((1,), (1,)), ((), ())), preferred_element_type=F32)


def _split3(x):
    hi = x.astype(BF16)
    r1 = x - hi.astype(F32)
    mid = r1.astype(BF16)
    lo = (r1 - mid.astype(F32)).astype(BF16)
    return hi, mid, lo


def _inproj_body(x_ref, g_ref, w_ref, ssd_ref, sb_ref, qm_ref):
    h = _rms(x_ref[...], g_ref[...]).astype(BF16)
    ssd_ref[...] = _dot(h, w_ref[:, 0:SSD_OUT])
    sb_ref[...] = _dot(h, w_ref[:, SSD_OUT:SSD_OUT + SB_OUT]).astype(BF16)
    qm_ref[...] = _dot(h, w_ref[:, SSD_OUT + SB_OUT:IN_COLS])


def _in_proj(x2, g, w, tm):
    t = x2.shape[0]
    return pl.pallas_call(
        _inproj_body,
        grid=(t // tm,),
        in_specs=[pl.BlockSpec((tm, D_MODEL), lambda i: (i, 0)),
                  pl.BlockSpec((1, D_MODEL), lambda i: (0, 0)),
                  pl.BlockSpec((D_MODEL, IN_COLS), lambda i: (0, 0))],
        out_specs=[pl.BlockSpec((tm, SSD_OUT), lambda i: (i, 0)),
                   pl.BlockSpec((tm, SB_OUT), lambda i: (i, 0)),
                   pl.BlockSpec((tm, MEM_WIDTH), lambda i: (i, 0))],
        out_shape=[jax.ShapeDtypeStruct((t, SSD_OUT), F32),
                   jax.ShapeDtypeStruct((t, SB_OUT), BF16),
                   jax.ShapeDtypeStruct((t, MEM_WIDTH), F32)],
        compiler_params=_params(("parallel",), VMEM_LIMIT),
        name="in_proj",
    )(x2, g, w)


def _ssd_body(u_ref, cw_ref, cb_ref, dtb_ref, alog_ref, dsk_ref, ng_ref, y_ref, xpad_ref, st_ref):
    q_len = SSD_CHUNK
    c = pl.program_id(1)

    @pl.when(c == 0)
    def _():
        xpad_ref[0:8, :] = jnp.zeros((8, SSD_CONV_CH), F32)
        st_ref[...] = jnp.zeros(st_ref.shape, F32)

    xpad_ref[8:8 + q_len, :] = u_ref[:, SSD_WIDTH:SSD_WIDTH + SSD_CONV_CH]
    conv = cb_ref[...]
    for k in range(SSD_CONV):
        off = 8 - (SSD_CONV - 1) + k
        conv = conv + cw_ref[k:k + 1, :] * xpad_ref[off:off + q_len, :]
    xpad_ref[0:8, :] = xpad_ref[q_len:q_len + 8, :]
    xbc = conv * _sigmoid(conv)

    xs = xbc[:, 0:SSD_WIDTH]
    bm = xbc[:, SSD_WIDTH:SSD_WIDTH + LANES]
    cm = xbc[:, SSD_WIDTH + LANES:SSD_WIDTH + 2 * LANES]

    udt = u_ref[:, SSD_WIDTH + SSD_CONV_CH:SSD_OUT] + dtb_ref[...]
    dt = jnp.maximum(udt, 0.0) + jnp.log1p(jnp.exp(-jnp.abs(udt)))
    adt = dt * (-jnp.exp(alog_ref[...]))

    row = lax.broadcasted_iota(jnp.int32, (q_len, q_len), 0)
    lane = lax.broadcasted_iota(jnp.int32, (q_len, q_len), 1)
    causal = row >= lane
    lo_half = lane < HEAD_DIM
    tri = jnp.where(causal, 1.0, 0.0).astype(BF16)
    hi, mid, lo = _split3(adt)
    acs = _dot(tri, hi) + _dot(tri, mid) + _dot(tri, lo)
    acs_t = acs.T
    last = acs[q_len - 1:q_len, :]
    dec_s = jnp.exp(last - acs)
    eacs = jnp.exp(acs)
    cdec = jnp.exp(last)

    bm_b = bm.astype(BF16)
    bt_b = bm.T.astype(BF16)
    cmask = [jnp.where(lo_half, cm, 0.0).astype(BF16), jnp.where(lo_half, 0.0, cm).astype(BF16)]
    cb = [_dot_nt(cmask[g], bm_b) for g in range(SSD_GROUPS)]

    def col_pair(m, h0, h1):
        return jnp.where(lo_half, m[:, h0:h0 + 1], m[:, h1:h1 + 1])

    def lmat(h):
        seg = acs[:, h:h + 1] - acs_t[h:h + 1, :]
        return jnp.exp(jnp.where(causal, seg, NEG_INF))

    heads_per_group = SSD_HEADS // SSD_GROUPS
    ys = []
    for p in range(SSD_HEADS // 2):
        h0, h1 = 2 * p, 2 * p + 1
        g0, g1 = h0 // heads_per_group, h1 // heads_per_group
        xp = xs[:, p * LANES:(p + 1) * LANES]
        xdt = xp * col_pair(dt, h0, h1)
        xdt_b = xdt.astype(BF16)
        m0 = (cb[g0] * lmat(h0)).astype(BF16)
        m1 = (cb[g1] * lmat(h1)).astype(BF16)
        y_diag = jnp.where(lo_half, _dot(m0, xdt_b), _dot(m1, xdt_b))
        st = st_ref[p]
        st_b = st.astype(BF16)
        if g0 == g1:
            y_off = _dot(cmask[g0], st_b)
        else:
            y_off = jnp.where(lo_half, _dot(cmask[g0], st_b), _dot(cmask[g1], st_b))
        y_off = y_off * col_pair(eacs, h0, h1)
        dx = (xdt * col_pair(dec_s, h0, h1)).astype(BF16)
        st_ref[p] = st * col_pair(cdec, h0, h1) + _dot(bt_b, dx)
        ys.append(y_diag + y_off + xp * dsk_ref[:, p * LANES:(p + 1) * LANES])
    y = jnp.concatenate(ys, axis=1)
    z = u_ref[:, 0:SSD_WIDTH]
    y_ref[...] = _rms(y * (z * _sigmoid(z)), ng_ref[...]).astype(BF16)


def _ssd(u3, conv_w, conv_b, dt_bias, a_log, d_skip_cols, norm_g):
    b, s, _ = u3.shape
    nc = s // SSD_CHUNK
    const = lambda shape: pl.BlockSpec(shape, lambda i, j: (0,) * len(shape))
    return pl.pallas_call(
        _ssd_body,
        grid=(b, nc),
        in_specs=[pl.BlockSpec((None, SSD_CHUNK, SSD_OUT), lambda i, j: (i, j, 0)),
                  const((SSD_CONV, SSD_CONV_CH)), const((1, SSD_CONV_CH)),
                  const((1, DT_PAD)), const((1, DT_PAD)),
                  const((1, SSD_WIDTH)), const((1, SSD_WIDTH))],
        out_specs=pl.BlockSpec((None, SSD_CHUNK, SSD_WIDTH), lambda i, j: (i, j, 0)),
        out_shape=jax.ShapeDtypeStruct((b, s, SSD_WIDTH), BF16),
        scratch_shapes=[pltpu.VMEM((SSD_CHUNK + 8, SSD_CONV_CH), F32),
                        pltpu.VMEM((SSD_HEADS // 2, LANES, LANES), F32)],
        compiler_params=_params(("parallel", "arbitrary")),
        name="ssd",
    )(u3, conv_w, conv_b, dt_bias, a_log, d_skip_cols, norm_g)


def _sb_body(q_ref, k_ref, v_ref, g_ref, o_ref, acc_ref, r_ref, z_ref, lb_ref, cs_ref, rs_ref, *, tile):
    m = pl.program_id(1)
    npair = SB_HEADS // 2
    row = lax.broadcasted_iota(jnp.int32, (tile, tile), 0)
    col = lax.broadcasted_iota(jnp.int32, (tile, tile), 1)
    strict = col < row
    upper = jnp.where(row > col, 1.0, 0.0).astype(BF16)
    ucat = jnp.concatenate([upper, upper], axis=0)
    lo_half = lax.broadcasted_iota(jnp.int32, (tile, LANES), 1) < HEAD_DIM
    qh = []
    for p in range(npair):
        q = q_ref[:, p * LANES:(p + 1) * LANES] * jnp.asarray(HEAD_DIM ** -0.5, BF16)
        zero = jnp.zeros_like(q)
        qh += [jnp.where(lo_half, q, zero), jnp.where(lo_half, zero, q)]
    acc_ref[...] = jnp.zeros(acc_ref.shape, F32)
    r_ref[...] = jnp.zeros(r_ref.shape, F32)

    def key_tile(j, diag):
        start = pl.multiple_of(j * tile, tile)

        def scores(h):
            kb = k_ref[pl.ds(start, tile), (h // 2) * LANES:(h // 2 + 1) * LANES]
            z_ref[h % 2] = _dot_nt(qh[h], kb)

        def log_terms(h):
            z = z_ref[h % 2]
            log_beta = jnp.minimum(z, 0.0) - jnp.log(1.0 + jnp.exp(-jnp.abs(z)))
            log_keep = log_beta - z
            if diag:
                log_keep = jnp.where(strict, log_keep, 0.0)
            hi = log_keep.astype(BF16)
            lo = (log_keep - hi.astype(F32)).astype(BF16)
            lb_ref[h % 2] = log_beta
            rs_ref[h % 2] = jnp.broadcast_to(jnp.sum(log_keep, axis=-1, keepdims=True), (tile, LANES))
            cs_ref[h % 2] = _dot(jnp.concatenate([hi, lo], axis=1), ucat)

        def weights(h):
            r = r_ref[h]
            w = jnp.exp(lb_ref[h % 2] + cs_ref[h % 2] + jnp.concatenate([r] * (tile // LANES), axis=1))
            if diag:
                w = jnp.where(strict, w, 0.0)
            r_ref[h] = r + rs_ref[h % 2]
            vb = v_ref[pl.ds(start, tile), (h // 2) * LANES:(h // 2 + 1) * LANES]
            out = _dot(w.astype(BF16), vb)
            keep = lo_half if h % 2 == 0 else jnp.logical_not(lo_half)
            acc_ref[h // 2] += jnp.where(keep, out, 0.0)

        scores(0)
        for h in range(SB_HEADS):
            if h + 1 < SB_HEADS:
                scores(h + 1)
            log_terms(h)
            if h >= 1:
                weights(h - 1)
        weights(SB_HEADS - 1)

    key_tile(m, True)

    def body(t, carry):
        key_tile(m - 1 - t, False)
        return carry

    lax.fori_loop(0, m, body, 0)
    y = jnp.concatenate([acc_ref[p] for p in range(npair)], axis=1)
    o_ref[...] = _rms(y, g_ref[...]).astype(BF16)


def _sb_attn(sb3, out_g, tile):
    b, s, _ = sb3.shape
    return pl.pallas_call(
        functools.partial(_sb_body, tile=tile),
        grid=(b, s // tile),
        in_specs=[pl.BlockSpec((None, tile, SB_WIDTH), lambda bi, i: (bi, i, 0)),
                  pl.BlockSpec((None, s, SB_WIDTH), lambda bi, i: (bi, 0, 1)),
                  pl.BlockSpec((None, s, SB_WIDTH), lambda bi, i: (bi, 0, 2)),
                  pl.BlockSpec((1, SB_WIDTH), lambda bi, i: (0, 0))],
        out_specs=pl.BlockSpec((None, tile, SB_WIDTH), lambda bi, i: (bi, i, 0)),
        out_shape=jax.ShapeDtypeStruct((b, s, SB_WIDTH), BF16),
        scratch_shapes=[pltpu.VMEM((SB_HEADS // 2, tile, LANES), F32),
                        pltpu.VMEM((SB_HEADS, tile, LANES), F32),
                        pltpu.VMEM((2, tile, tile), F32), pltpu.VMEM((2, tile, tile), F32),
                        pltpu.VMEM((2, tile, tile), F32), pltpu.VMEM((2, tile, LANES), F32)],
        compiler_params=_params(("parallel", "arbitrary"), VMEM_LIMIT),
        name="sb_attn",
    )(sb3, sb3, sb3, out_g)


def _head_rms(x, g):
    lane = lax.broadcasted_iota(jnp.int32, x.shape, 1)
    sq = x * x
    inv = jnp.zeros_like(x)
    for h in range(MEM_HEADS):
        in_head = (lane >= h * HEAD_DIM) & (lane < (h + 1) * HEAD_DIM)
        ms = jnp.sum(jnp.where(in_head, sq, 0.0), axis=-1, keepdims=True) * (1.0 / HEAD_DIM)
        inv = jnp.where(in_head, lax.rsqrt(ms + EPS), inv)
    return x * inv * g


def _memkv_body(m_ref, g_ref, w_ref, kg_ref, k_ref, v_ref):
    h = _rms(m_ref[...], g_ref[...]).astype(BF16)
    kv = _dot(h, w_ref[...])
    k_ref[...] = _head_rms(kv[:, 0:MEM_WIDTH], kg_ref[...]).astype(BF16)
    v_ref[...] = kv[:, MEM_WIDTH:2 * MEM_WIDTH].astype(BF16)


def _mem_kv(mem, g, w, kg_cols):
    b, m, _ = mem.shape
    return pl.pallas_call(
        _memkv_body,
        grid=(b,),
        in_specs=[pl.BlockSpec((None, m, D_MODEL), lambda i: (i, 0, 0)),
                  pl.BlockSpec((1, D_MODEL), lambda i: (0, 0)),
                  pl.BlockSpec((D_MODEL, 2 * MEM_WIDTH), lambda i: (0, 0)),
                  pl.BlockSpec((1, MEM_WIDTH), lambda i: (0, 0))],
        out_specs=[pl.BlockSpec((None, m, MEM_WIDTH), lambda i: (i, 0, 0)),
                   pl.BlockSpec((None, m, MEM_WIDTH), lambda i: (i, 0, 0))],
        out_shape=[jax.ShapeDtypeStruct((b, m, MEM_WIDTH), BF16),
                   jax.ShapeDtypeStruct((b, m, MEM_WIDTH), BF16)],
        compiler_params=_params(("parallel",)),
        name="mem_kv",
    )(mem, g, w, kg_cols)


def _memattn_body(q_ref, k_ref, v_ref, qg_ref, og_ref, o_ref):
    qn = _head_rms(q_ref[...], qg_ref[...]).astype(BF16)
    lane = lax.broadcasted_iota(jnp.int32, qn.shape, 1)
    k = k_ref[...]
    v = v_ref[...]
    out = jnp.zeros(qn.shape, F32)
    for h in range(MEM_HEADS):
        in_head = (lane >= h * HEAD_DIM) & (lane < (h + 1) * HEAD_DIM)
        s = _dot_nt(jnp.where(in_head, qn, jnp.zeros_like(qn)), k) * (HEAD_DIM ** -0.5)
        e = jnp.exp(s - jnp.max(s, axis=-1, keepdims=True))
        p = e / jnp.sum(e, axis=-1, keepdims=True)
        out = jnp.where(in_head, _dot(p.astype(BF16), v), out)
    o_ref[...] = _rms(out, og_ref[...]).astype(BF16)


def _mem_attn(qm3, mk, mv, qg_cols, og, tq):
    b, s, _ = qm3.shape
    m = mk.shape[1]
    return pl.pallas_call(
        _memattn_body,
        grid=(b, s // tq),
        in_specs=[pl.BlockSpec((None, tq, MEM_WIDTH), lambda i, j: (i, j, 0)),
                  pl.BlockSpec((None, m, MEM_WIDTH), lambda i, j: (i, 0, 0)),
                  pl.BlockSpec((None, m, MEM_WIDTH), lambda i, j: (i, 0, 0)),
                  pl.BlockSpec((1, MEM_WIDTH), lambda i, j: (0, 0)),
                  pl.BlockSpec((1, MEM_WIDTH), lambda i, j: (0, 0))],
        out_specs=pl.BlockSpec((None, tq, MEM_WIDTH), lambda i, j: (i, j, 0)),
        out_shape=jax.ShapeDtypeStruct((b, s, MEM_WIDTH), BF16),
        compiler_params=_params(("parallel", "parallel")),
        name="mem_attn",
    )(qm3, mk, mv, qg_cols, og)


def _outproj_body(x_ref, ys_ref, yb_ref, ym_ref, w_ref, o_ref):
    acc = _dot(ys_ref[...], w_ref[0:SSD_WIDTH, :])
    acc = acc + _dot(yb_ref[...], w_ref[SSD_WIDTH:SSD_WIDTH + SB_WIDTH, :])
    acc = acc + _dot(ym_ref[...], w_ref[SSD_WIDTH + SB_WIDTH:D_MODEL, :])
    o_ref[...] = x_ref[...] + acc


def _out_proj(x2, y_ssd, y_sb, y_mem, w, tm):
    t = x2.shape[0]
    rows = lambda width: pl.BlockSpec((tm, width), lambda i: (i, 0))
    return pl.pallas_call(
        _outproj_body,
        grid=(t // tm,),
        in_specs=[rows(D_MODEL), rows(SSD_WIDTH), rows(SB_WIDTH), rows(MEM_WIDTH),
                  pl.BlockSpec((D_MODEL, D_MODEL), lambda i: (0, 0))],
        out_specs=rows(D_MODEL),
        out_shape=jax.ShapeDtypeStruct((t, D_MODEL), F32),
        compiler_params=_params(("parallel",), VMEM_LIMIT),
        name="out_proj",
    )(x2, y_ssd, y_sb, y_mem, w)


def _top_values(scores, k, want_rank=False):
    work = scores
    vals = []
    rank = jnp.full(scores.shape, 2.0 * k, F32) if want_rank else None
    for r in range(k):
        m = jnp.max(work, axis=0, keepdims=True)
        vals.append(m)
        if want_rank or r + 1 < k:
            hit = work == m
            if want_rank:
                rank = jnp.where(hit, float(r), rank)
            if r + 1 < k:
                work = jnp.where(hit, NEG_INF, work)
    return vals, rank


def _bf16_pair(v):
    bits = lax.bitcast_convert_type(v.astype(BF16).astype(F32), jnp.uint32)
    return lax.bitcast_convert_type(bits | (bits >> 16), F32)


def _peer_pre_body(x_ref, g_ref, wqt_ref, keys_ref, h2t_ref, ce_ref, rk_ref):
    topk = PEER_TOPK
    half = PEER_DKEY // 2
    h2 = _rms(x_ref[...], g_ref[...])
    h2t = h2.T.astype(BF16)
    h2t_ref[...] = h2t
    qt = _dot(wqt_ref[...], h2t).astype(BF16)
    for h in range(PEER_HEADS):
        s1 = _dot(keys_ref[h, 0], qt[(2 * h) * half:(2 * h + 1) * half, :])
        s2 = _dot(keys_ref[h, 1], qt[(2 * h + 1) * half:(2 * h + 2) * half, :])
        a, _ = _top_values(s1, topk)
        b, rank2 = _top_values(s2, topk, want_rank=True)
        cands = [a[p] + b[q] for p in range(topk) for q in range(topk // (p + 1))]
        pad = (-len(cands)) % 8
        cand = jnp.concatenate(cands + [jnp.full_like(a[0], NEG_INF)] * pad, axis=0)
        tau = _top_values(cand, topk)[0][topk - 1]
        b_sorted = jnp.concatenate(b, axis=0)
        eb_sorted = jnp.exp(b_sorted - b[0])
        zsum = jnp.zeros_like(tau)
        cnt1 = jnp.zeros_like(s1)
        for p in range(topk):
            sel = (a[p] + b_sorted) >= tau
            cnt_p = jnp.sum(jnp.where(sel, 1.0, 0.0), axis=0, keepdims=True)
            zsum = zsum + jnp.exp(a[p] - a[0]) * jnp.sum(jnp.where(sel, eb_sorted, 0.0), axis=0, keepdims=True)
            cnt1 = jnp.where(s1 == a[p], cnt_p, cnt1)
        ce_ref[0, h] = _bf16_pair(cnt1)
        ce_ref[1, h] = _bf16_pair(jnp.exp(s1 - a[0]))
        rk_ref[0, h] = rank2.astype(BF16)
        rk_ref[1, h] = (jnp.exp(s2 - b[0]) * (0.5 / zsum)).astype(BF16)


def _peer_pre(x2, g, wqt, keys, tb):
    t = x2.shape[0]
    info = lambda: pl.BlockSpec((2, PEER_HEADS, PEER_NKEYS, tb), lambda i: (0, 0, 0, i))
    return pl.pallas_call(
        _peer_pre_body,
        grid=(t // tb,),
        in_specs=[pl.BlockSpec((tb, D_MODEL), lambda i: (i, 0)),
                  pl.BlockSpec((1, D_MODEL), lambda i: (0, 0)),
                  pl.BlockSpec((D_MODEL, D_MODEL), lambda i: (0, 0)),
                  pl.BlockSpec((PEER_HEADS, 2, PEER_NKEYS, PEER_DKEY // 2), lambda i: (0, 0, 0, 0))],
        out_specs=[pl.BlockSpec((D_MODEL, tb), lambda i: (0, i)), info(), info()],
        out_shape=[jax.ShapeDtypeStruct((D_MODEL, t), BF16),
                   jax.ShapeDtypeStruct((2, PEER_HEADS, PEER_NKEYS, t), F32),
                   jax.ShapeDtypeStruct((2, PEER_HEADS, PEER_NKEYS, t), BF16)],
        compiler_params=_params(("parallel",), VMEM_LIMIT),
        name="peer_pre",
    )(x2, g, wqt, keys)


def _peer_main_body(h2t_ref, ce_ref, rk_ref, u_ref, vt_ref, x_ref, o_ref, acc_ref, at_ref, pt_ref, *, eb, stage):
    e = pl.program_id(1)
    n_stage = eb // stage
    keys_per_stage = stage // PEER_NKEYS
    tb = h2t_ref.shape[1]
    zero = jnp.zeros((PEER_NKEYS, tb), BF16)

    @pl.when(e == 0)
    def _():
        acc_ref[...] = jnp.zeros(acc_ref.shape, F32)

    def row_tile(kind, h, i):
        row = ce_ref[kind, h, pl.ds(i, 1), :]
        return pltpu.bitcast(jnp.broadcast_to(row, (PEER_NKEYS // 2, tb)), BF16)

    def activations(s):
        at_ref[s % 2] = _dot(u_ref[s * stage:(s + 1) * stage, :], h2t_ref[...])

    def weighting(s):
        for k in range(keys_per_stage):
            i = (e * n_stage + s) * keys_per_stage + k
            rows = slice(k * PEER_NKEYS, (k + 1) * PEER_NKEYS)
            wd = None
            for h in range(PEER_HEADS):
                term = jnp.where(rk_ref[0, h] < row_tile(0, h, i), rk_ref[1, h], zero) * row_tile(1, h, i)
                wd = term if wd is None else wd + term
            at = at_ref[s % 2, rows, :]
            act = (at * (1.0 + lax.erf(at * math.sqrt(0.5)))).astype(BF16)
            pt_ref[s % 2, rows, :] = wd * act

    def projection(s):
        acc_ref[...] += _dot(vt_ref[:, s * stage:(s + 1) * stage], pt_ref[s % 2])

    activations(0)
    for s in range(n_stage):
        if s + 1 < n_stage:
            activations(s + 1)
        weighting(s)
        if s >= 1:
            projection(s - 1)
    projection(n_stage - 1)

    @pl.when(e == pl.num_programs(1) - 1)
    def _():
        o_ref[...] = x_ref[...] + acc_ref[...].T


def _peer_main(x2, h2t, ce, rk, u_b, vt_b, tb, eb):
    t = x2.shape[0]
    info = lambda: pl.BlockSpec((2, PEER_HEADS, PEER_NKEYS, tb), lambda i, e: (0, 0, 0, i))
    stage = 2 * PEER_NKEYS
    return pl.pallas_call(
        functools.partial(_peer_main_body, eb=eb, stage=stage),
        grid=(t // tb, PEER_EXPERTS // eb),
        in_specs=[pl.BlockSpec((D_MODEL, tb), lambda i, e: (0, i)), info(), info(),
                  pl.BlockSpec((eb, D_MODEL), lambda i, e: (e, 0)),
                  pl.BlockSpec((D_MODEL, eb), lambda i, e: (0, e)),
                  pl.BlockSpec((tb, D_MODEL), lambda i, e: (i, 0))],
        out_specs=pl.BlockSpec((tb, D_MODEL), lambda i, e: (i, 0)),
        out_shape=jax.ShapeDtypeStruct((t, D_MODEL), F32),
        scratch_shapes=[pltpu.VMEM((D_MODEL, tb), F32), pltpu.VMEM((2, stage, tb), F32),
                        pltpu.VMEM((2, stage, tb), BF16)],
        compiler_params=_params(("parallel", "arbitrary"), VMEM_LIMIT),
        name="peer_main",
    )(h2t, ce, rk, u_b, vt_b, x2)


def _pick(n, pref):
    return pref if n % pref == 0 else n


def _permute_w_in(w):
    pts = [0]
    for width in IN_SPLITS:
        pts.append(pts[-1] + width)
    z, xbc, dtw, q, k, v, qm = [w[:, pts[n]:pts[n + 1]] for n in range(len(IN_SPLITS))]
    dtw = jnp.pad(dtw, ((0, 0), (0, DT_PAD - SSD_HEADS)))
    return jnp.concatenate([z, xbc, dtw, q, k, v, qm], axis=1).astype(BF16)


def _pad_row(v, width):
    return jnp.pad(v, (0, width - v.shape[0])).reshape(1, width)


def kernel(x, mem, norm1_g, w_in, conv_w, conv_b, dt_bias, a_log, d_skip, ssd_norm_g, sb_out_g, mem_norm_g,
           w_mem_kv, mq_norm_g, mk_norm_g, mem_out_g, w_out, norm2_g, peer_wq, peer_keys, peer_u, peer_v):
    bsz, seqlen, _ = x.shape
    t = bsz * seqlen
    depth = w_in.shape[0]
    tm = _pick(t, 512)
    tq = _pick(seqlen, 512)
    tb = _pick(t, 512)
    eb = 2048
    row = lambda v: v.reshape(1, -1)
    x2 = x.reshape(t, D_MODEL)
    for l in range(depth):
        u_ssd, u_sb, u_qm = _in_proj(x2, row(norm1_g[l]), _permute_w_in(w_in[l]), tm)
        y_ssd = _ssd(u_ssd.reshape(bsz, seqlen, SSD_OUT), conv_w[l], row(conv_b[l]),
                     _pad_row(dt_bias[l], DT_PAD), _pad_row(a_log[l], DT_PAD),
                     row(jnp.repeat(d_skip[l], HEAD_DIM)), row(ssd_norm_g[l]))
        y_sb = _sb_attn(u_sb.reshape(bsz, seqlen, SB_OUT), row(sb_out_g[l]), _pick(seqlen, 256))
        mk, mv = _mem_kv(mem, row(mem_norm_g[l]), w_mem_kv[l].astype(BF16),
                         row(jnp.tile(mk_norm_g[l], MEM_HEADS)))
        y_mem = _mem_attn(u_qm.reshape(bsz, seqlen, MEM_WIDTH), mk, mv,
                          row(jnp.tile(mq_norm_g[l], MEM_HEADS)), row(mem_out_g[l]), tq)
        x2 = _out_proj(x2, y_ssd.reshape(t, SSD_WIDTH), y_sb.reshape(t, SB_WIDTH),
                       y_mem.reshape(t, MEM_WIDTH), w_out[l].astype(BF16), tm)
        h2t, ce, rk = _peer_pre(x2, row(norm2_g[l]), peer_wq[l].T.astype(BF16), peer_keys[l].astype(BF16), tb)
        x2 = _peer_main(x2, h2t, ce, rk, peer_u[l].astype(BF16), peer_v[l].T.astype(BF16), tb, eb)
    return x2.reshape(bsz, seqlen, D_MODEL)
```

```python
import functools
import math

import jax
import jax.numpy as jnp
from jax import lax
from jax.experimental import pallas as pl
from jax.experimental.pallas import tpu as pltpu

F32 = jnp.float32
BF16 = jnp.bfloat16

D_MODEL = 1024
HEAD_DIM = 64
SSD_HEADS = 6
SSD_WIDTH = SSD_HEADS * HEAD_DIM
SSD_GROUPS = 2
SSD_STATE = 64
SSD_CONV = 4
SSD_CHUNK = 128
SSD_CONV_CH = SSD_WIDTH + 2 * SSD_GROUPS * SSD_STATE
SB_HEADS = 6
SB_WIDTH = SB_HEADS * HEAD_DIM
SB_BLOCK = 128
MEM_HEADS = 4
MEM_WIDTH = MEM_HEADS * HEAD_DIM
IN_SPLITS = [SSD_WIDTH, SSD_CONV_CH, SSD_HEADS, SB_WIDTH, SB_WIDTH, SB_WIDTH, MEM_WIDTH]
PEER_HEADS = 8
PEER_NKEYS = 128
PEER_EXPERTS = PEER_NKEYS * PEER_NKEYS
PEER_DKEY = 128
PEER_TOPK = 16
EPS = 1e-6

LANES = 128
DT_PAD = LANES
SSD_OUT = SSD_WIDTH + SSD_CONV_CH + DT_PAD
SB_OUT = 3 * SB_WIDTH
IN_COLS = SSD_OUT + SB_OUT + MEM_WIDTH
VMEM_LIMIT = 48 * 1024 * 1024

NEG_INF = float("-inf")
SB_EXP_ZERO = -104.0


def _params(sem, vmem=None, flags=None):
    return pltpu.CompilerParams(dimension_semantics=sem, vmem_limit_bytes=vmem, flags=flags)


def _rms(x, g):
    ms = jnp.mean(x * x, axis=-1, keepdims=True)
    return x * lax.rsqrt(ms + EPS) * g


def _sigmoid(x):
    return 1.0 / (1.0 + jnp.exp(-x))


def _dot(a, b):
    return jnp.dot(a, b, preferred_element_type=F32)


def _dot_nt(a, b):
    return lax.dot_general(a, b, (((1,), (1,)), ((), ())), preferred_element_type=F32)


def _split3(x):
    hi = x.astype(BF16)
    r1 = x - hi.astype(F32)
    mid = r1.astype(BF16)
    lo = (r1 - mid.astype(F32)).astype(BF16)
    return hi, mid, lo


def _inproj_body(x_ref, g_ref, w_ref, ssd_ref, sb_ref, qm_ref):
    h = _rms(x_ref[...], g_ref[...]).astype(BF16)
    ssd_ref[...] = _dot(h, w_ref[:, 0:SSD_OUT])
    sb_ref[...] = _dot(h, w_ref[:, SSD_OUT:SSD_OUT + SB_OUT]).astype(BF16)
    qm_ref[...] = _dot(h, w_ref[:, SSD_OUT + SB_OUT:IN_COLS])


def _in_proj(x2, g, w, tm):
    t = x2.shape[0]
    return pl.pallas_call(
        _inproj_body,
        grid=(t // tm,),
        in_specs=[pl.BlockSpec((tm, D_MODEL), lambda i: (i, 0)),
                  pl.BlockSpec((1, D_MODEL), lambda i: (0, 0)),
                  pl.BlockSpec((D_MODEL, IN_COLS), lambda i: (0, 0))],
        out_specs=[pl.BlockSpec((tm, SSD_OUT), lambda i: (i, 0)),
                   pl.BlockSpec((tm, SB_OUT), lambda i: (i, 0)),
                   pl.BlockSpec((tm, MEM_WIDTH), lambda i: (i, 0))],
        out_shape=[jax.ShapeDtypeStruct((t, SSD_OUT), F32),
                   jax.ShapeDtypeStruct((t, SB_OUT), BF16),
                   jax.ShapeDtypeStruct((t, MEM_WIDTH), F32)],
        compiler_params=_params(("parallel",), VMEM_LIMIT),
        name="in_proj",
    )(x2, g, w)


def _ssd_body(u_ref, cw_ref, cb_ref, dtb_ref, alog_ref, dsk_ref, ng_ref, y_ref, xpad_ref, st_ref):
    q_len = SSD_CHUNK
    c = pl.program_id(1)

    @pl.when(c == 0)
    def _():
        xpad_ref[0:8, :] = jnp.zeros((8, SSD_CONV_CH), F32)
        st_ref[...] = jnp.zeros(st_ref.shape, F32)

    xpad_ref[8:8 + q_len, :] = u_ref[:, SSD_WIDTH:SSD_WIDTH + SSD_CONV_CH]
    conv = cb_ref[...]
    for k in range(SSD_CONV):
        off = 8 - (SSD_CONV - 1) + k
        conv = conv + cw_ref[k:k + 1, :] * xpad_ref[off:off + q_len, :]
    xpad_ref[0:8, :] = xpad_ref[q_len:q_len + 8, :]
    xbc = conv * _sigmoid(conv)

    xs = xbc[:, 0:SSD_WIDTH]
    bm = xbc[:, SSD_WIDTH:SSD_WIDTH + LANES]
    cm = xbc[:, SSD_WIDTH + LANES:SSD_WIDTH + 2 * LANES]

    udt = u_ref[:, SSD_WIDTH + SSD_CONV_CH:SSD_OUT] + dtb_ref[...]
    dt = jnp.maximum(udt, 0.0) + jnp.log1p(jnp.exp(-jnp.abs(udt)))
    adt = dt * (-jnp.exp(alog_ref[...]))

    row = lax.broadcasted_iota(jnp.int32, (q_len, q_len), 0)
    lane = lax.broadcasted_iota(jnp.int32, (q_len, q_len), 1)
    causal = row >= lane
    lo_half = lane < HEAD_DIM
    tri = jnp.where(causal, 1.0, 0.0).astype(BF16)
    hi, mid, lo = _split3(adt)
    acs = _dot(tri, hi) + _dot(tri, mid) + _dot(tri, lo)
    acs_t = acs.T
    last = acs[q_len - 1:q_len, :]
    dec_s = jnp.exp(last - acs)
    eacs = jnp.exp(acs)
    cdec = jnp.exp(last)

    bm_b = bm.astype(BF16)
    bt_b = bm.T.astype(BF16)
    cmask = [jnp.where(lo_half, cm, 0.0).astype(BF16), jnp.where(lo_half, 0.0, cm).astype(BF16)]
    cb = [_dot_nt(cmask[g], bm_b) for g in range(SSD_GROUPS)]

    def col_pair(m, h0, h1):
        return jnp.where(lo_half, m[:, h0:h0 + 1], m[:, h1:h1 + 1])

    def lmat(h):
        seg = acs[:, h:h + 1] - acs_t[h:h + 1, :]
        return jnp.exp(jnp.where(causal, seg, NEG_INF))

    heads_per_group = SSD_HEADS // SSD_GROUPS
    ys = []
    for p in range(SSD_HEADS // 2):
        h0, h1 = 2 * p, 2 * p + 1
        g0, g1 = h0 // heads_per_group, h1 // heads_per_group
        xp = xs[:, p * LANES:(p + 1) * LANES]
        xdt = xp * col_pair(dt, h0, h1)
        xdt_b = xdt.astype(BF16)
        m0 = (cb[g0] * lmat(h0)).astype(BF16)
        m1 = (cb[g1] * lmat(h1)).astype(BF16)
        y_diag = jnp.where(lo_half, _dot(m0, xdt_b), _dot(m1, xdt_b))
        st = st_ref[p]
        st_b = st.astype(BF16)
        if g0 == g1:
            y_off = _dot(cmask[g0], st_b)
        else:
            y_off = jnp.where(lo_half, _dot(cmask[g0], st_b), _dot(cmask[g1], st_b))
        y_off = y_off * col_pair(eacs, h0, h1)
        dx = (xdt * col_pair(dec_s, h0, h1)).astype(BF16)
        st_ref[p] = st * col_pair(cdec, h0, h1) + _dot(bt_b, dx)
        ys.append(y_diag + y_off + xp * dsk_ref[:, p * LANES:(p + 1) * LANES])
    y = jnp.concatenate(ys, axis=1)
    z = u_ref[:, 0:SSD_WIDTH]
    y_ref[...] = _rms(y * (z * _sigmoid(z)), ng_ref[...]).astype(BF16)


def _ssd(u3, conv_w, conv_b, dt_bias, a_log, d_skip_cols, norm_g):
    b, s, _ = u3.shape
    nc = s // SSD_CHUNK
    const = lambda shape: pl.BlockSpec(shape, lambda i, j: (0,) * len(shape))
    return pl.pallas_call(
        _ssd_body,
        grid=(b, nc),
        in_specs=[pl.BlockSpec((None, SSD_CHUNK, SSD_OUT), lambda i, j: (i, j, 0)),
                  const((SSD_CONV, SSD_CONV_CH)), const((1, SSD_CONV_CH)),
                  const((1, DT_PAD)), const((1, DT_PAD)),
                  const((1, SSD_WIDTH)), const((1, SSD_WIDTH))],
        out_specs=pl.BlockSpec((None, SSD_CHUNK, SSD_WIDTH), lambda i, j: (i, j, 0)),
        out_shape=jax.ShapeDtypeStruct((b, s, SSD_WIDTH), BF16),
        scratch_shapes=[pltpu.VMEM((SSD_CHUNK + 8, SSD_CONV_CH), F32),
                        pltpu.VMEM((SSD_HEADS // 2, LANES, LANES), F32)],
        compiler_params=_params(("parallel", "arbitrary")),
        name="ssd",
    )(u3, conv_w, conv_b, dt_bias, a_log, d_skip_cols, norm_g)


def _sb_body(q_ref, k_ref, v_ref, g_ref, o_ref, acc_ref, r_ref, z_ref, lb_ref, cs_ref, rs_ref, *, tile):
    m = pl.program_id(1)
    npair = SB_HEADS // 2
    row = lax.broadcasted_iota(jnp.int32, (tile, tile), 0)
    col = lax.broadcasted_iota(jnp.int32, (tile, tile), 1)
    strict = col < row
    upper = jnp.where(row > col, 1.0, 0.0).astype(BF16)
    ucat = jnp.concatenate([upper, upper], axis=0)
    lo_half = lax.broadcasted_iota(jnp.int32, (tile, LANES), 1) < HEAD_DIM
    qh = []
    for p in range(npair):
        q = q_ref[:, p * LANES:(p + 1) * LANES] * jnp.asarray(HEAD_DIM ** -0.5, BF16)
        zero = jnp.zeros_like(q)
        qh += [jnp.where(lo_half, q, zero), jnp.where(lo_half, zero, q)]
    acc_ref[...] = jnp.zeros(acc_ref.shape, F32)
    r_ref[...] = jnp.zeros(r_ref.shape, F32)

    def key_tile(j, diag):
        start = pl.multiple_of(j * tile, tile)

        def scores(h):
            kb = k_ref[pl.ds(start, tile), (h // 2) * LANES:(h // 2 + 1) * LANES]
            z_ref[h % 2] = _dot_nt(qh[h], kb)

        def log_terms(h):
            z = z_ref[h % 2]
            log_beta = jnp.minimum(z, 0.0) - jnp.log(1.0 + jnp.exp(-jnp.abs(z)))
            log_keep = log_beta - z
            if diag:
                log_keep = jnp.where(strict, log_keep, 0.0)
            hi = log_keep.astype(BF16)
            lo = (log_keep - hi.astype(F32)).astype(BF16)
            lb_ref[h % 2] = log_beta
            rs_ref[h % 2] = jnp.broadcast_to(jnp.sum(log_keep, axis=-1, keepdims=True), (tile, LANES))
            cs_ref[h % 2] = _dot(jnp.concatenate([hi, lo], axis=1), ucat)

        def weights(h):
            r = r_ref[h]
            w = jnp.exp(lb_ref[h % 2] + cs_ref[h % 2] + jnp.concatenate([r] * (tile // LANES), axis=1))
            if diag:
                w = jnp.where(strict, w, 0.0)
            r_ref[h] = r + rs_ref[h % 2]
            vb = v_ref[pl.ds(start, tile), (h // 2) * LANES:(h // 2 + 1) * LANES]
            out = _dot(w.astype(BF16), vb)
            keep = lo_half if h % 2 == 0 else jnp.logical_not(lo_half)
            acc_ref[h // 2] += jnp.where(keep, out, 0.0)

        scores(0)
        for h in range(SB_HEADS):
            if h + 1 < SB_HEADS:
                scores(h + 1)
            log_terms(h)
            if h >= 1:
                weights(h - 1)
        weights(SB_HEADS - 1)

    key_tile(m, True)

    def more_weight():
        return jnp.max(r_ref[...]) > SB_EXP_ZERO

    def body(carry):
        t, _ = carry
        key_tile(m - 1 - t, False)
        return t + 1, more_weight()

    lax.while_loop(lambda c: jnp.logical_and(c[0] < m, c[1]), body, (jnp.int32(0), more_weight()))
    y = jnp.concatenate([acc_ref[p] for p in range(npair)], axis=1)
    o_ref[...] = _rms(y, g_ref[...]).astype(BF16)


def _sb_attn(sb3, out_g, tile):
    b, s, _ = sb3.shape
    return pl.pallas_call(
        functools.partial(_sb_body, tile=tile),
        grid=(b, s // tile),
        in_specs=[pl.BlockSpec((None, tile, SB_WIDTH), lambda bi, i: (bi, i, 0)),
                  pl.BlockSpec((None, s, SB_WIDTH), lambda bi, i: (bi, 0, 1)),
                  pl.BlockSpec((None, s, SB_WIDTH), lambda bi, i: (bi, 0, 2)),
                  pl.BlockSpec((1, SB_WIDTH), lambda bi, i: (0, 0))],
        out_specs=pl.BlockSpec((None, tile, SB_WIDTH), lambda bi, i: (bi, i, 0)),
        out_shape=jax.ShapeDtypeStruct((b, s, SB_WIDTH), BF16),
        scratch_shapes=[pltpu.VMEM((SB_HEADS // 2, tile, LANES), F32),
                        pltpu.VMEM((SB_HEADS, tile, LANES), F32),
                        pltpu.VMEM((2, tile, tile), F32), pltpu.VMEM((2, tile, tile), F32),
                        pltpu.VMEM((2, tile, tile), F32), pltpu.VMEM((2, tile, LANES), F32)],
        compiler_params=_params(("parallel", "arbitrary"), VMEM_LIMIT),
        name="sb_attn",
    )(sb3, sb3, sb3, out_g)


def _head_rms(x, g):
    lane = lax.broadcasted_iota(jnp.int32, x.shape, 1)
    sq = x * x
    inv = jnp.zeros_like(x)
    for h in range(MEM_HEADS):
        in_head = (lane >= h * HEAD_DIM) & (lane < (h + 1) * HEAD_DIM)
        ms = jnp.sum(jnp.where(in_head, sq, 0.0), axis=-1, keepdims=True) * (1.0 / HEAD_DIM)
        inv = jnp.where(in_head, lax.rsqrt(ms + EPS), inv)
    return x * inv * g


def _memkv_body(m_ref, g_ref, w_ref, kg_ref, k_ref, v_ref):
    h = _rms(m_ref[...], g_ref[...]).astype(BF16)
    kv = _dot(h, w_ref[...])
    k_ref[...] = _head_rms(kv[:, 0:MEM_WIDTH], kg_ref[...]).astype(BF16)
    v_ref[...] = kv[:, MEM_WIDTH:2 * MEM_WIDTH].astype(BF16)


def _mem_kv(mem, g, w, kg_cols):
    b, m, _ = mem.shape
    return pl.pallas_call(
        _memkv_body,
        grid=(b,),
        in_specs=[pl.BlockSpec((None, m, D_MODEL), lambda i: (i, 0, 0)),
                  pl.BlockSpec((1, D_MODEL), lambda i: (0, 0)),
                  pl.BlockSpec((D_MODEL, 2 * MEM_WIDTH), lambda i: (0, 0)),
                  pl.BlockSpec((1, MEM_WIDTH), lambda i: (0, 0))],
        out_specs=[pl.BlockSpec((None, m, MEM_WIDTH), lambda i: (i, 0, 0)),
                   pl.BlockSpec((None, m, MEM_WIDTH), lambda i: (i, 0, 0))],
        out_shape=[jax.ShapeDtypeStruct((b, m, MEM_WIDTH), BF16),
                   jax.ShapeDtypeStruct((b, m, MEM_WIDTH), BF16)],
        compiler_params=_params(("parallel",)),
        name="mem_kv",
    )(mem, g, w, kg_cols)


def _memattn_body(q_ref, k_ref, v_ref, qg_ref, og_ref, o_ref):
    qn = _head_rms(q_ref[...], qg_ref[...]).astype(BF16)
    lane = lax.broadcasted_iota(jnp.int32, qn.shape, 1)
    k = k_ref[...]
    v = v_ref[...]
    out = jnp.zeros(qn.shape, F32)
    for h in range(MEM_HEADS):
        in_head = (lane >= h * HEAD_DIM) & (lane < (h + 1) * HEAD_DIM)
        s = _dot_nt(jnp.where(in_head, qn, jnp.zeros_like(qn)), k) * (HEAD_DIM ** -0.5)
        e = jnp.exp(s - jnp.max(s, axis=-1, keepdims=True))
        p = e / jnp.sum(e, axis=-1, keepdims=True)
        out = jnp.where(in_head, _dot(p.astype(BF16), v), out)
    o_ref[...] = _rms(out, og_ref[...]).astype(BF16)


def _mem_attn(qm3, mk, mv, qg_cols, og, tq):
    b, s, _ = qm3.shape
    m = mk.shape[1]
    return pl.pallas_call(
        _memattn_body,
        grid=(b, s // tq),
        in_specs=[pl.BlockSpec((None, tq, MEM_WIDTH), lambda i, j: (i, j, 0)),
                  pl.BlockSpec((None, m, MEM_WIDTH), lambda i, j: (i, 0, 0)),
                  pl.BlockSpec((None, m, MEM_WIDTH), lambda i, j: (i, 0, 0)),
                  pl.BlockSpec((1, MEM_WIDTH), lambda i, j: (0, 0)),
                  pl.BlockSpec((1, MEM_WIDTH), lambda i, j: (0, 0))],
        out_specs=pl.BlockSpec((None, tq, MEM_WIDTH), lambda i, j: (i, j, 0)),
        out_shape=jax.ShapeDtypeStruct((b, s, MEM_WIDTH), BF16),
        compiler_params=_params(("parallel", "parallel")),
        name="mem_attn",
    )(qm3, mk, mv, qg_cols, og)


def _outproj_body(x_ref, ys_ref, yb_ref, ym_ref, w_ref, o_ref):
    acc = _dot(ys_ref[...], w_ref[0:SSD_WIDTH, :])
    acc = acc + _dot(yb_ref[...], w_ref[SSD_WIDTH:SSD_WIDTH + SB_WIDTH, :])
    acc = acc + _dot(ym_ref[...], w_ref[SSD_WIDTH + SB_WIDTH:D_MODEL, :])
    o_ref[...] = x_ref[...] + acc


def _out_proj(x2, y_ssd, y_sb, y_mem, w, tm):
    t = x2.shape[0]
    rows = lambda width: pl.BlockSpec((tm, width), lambda i: (i, 0))
    return pl.pallas_call(
        _outproj_body,
        grid=(t // tm,),
        in_specs=[rows(D_MODEL), rows(SSD_WIDTH), rows(SB_WIDTH), rows(MEM_WIDTH),
                  pl.BlockSpec((D_MODEL, D_MODEL), lambda i: (0, 0))],
        out_specs=rows(D_MODEL),
        out_shape=jax.ShapeDtypeStruct((t, D_MODEL), F32),
        compiler_params=_params(("parallel",), VMEM_LIMIT),
        name="out_proj",
    )(x2, y_ssd, y_sb, y_mem, w)


def _top_values(scores, k):
    work = scores
    vals = []
    for r in range(k):
        m = jnp.max(work, axis=0, keepdims=True)
        vals.append(m)
        if r + 1 < k:
            work = jnp.where(work == m, NEG_INF, work)
    return vals


def _sorting_network(n):
    pairs = []
    p = 1
    while p < n:
        k = p
        while k >= 1:
            for j in range(k % p, n - k, 2 * k):
                for i in range(min(k, n - j - k)):
                    if (i + j) // (2 * p) == (i + j + k) // (2 * p):
                        pairs.append((i + j, i + j + k))
            k //= 2
        p *= 2
    return pairs


def _top_values_sorted(scores, k):
    sub = 8
    blocks = [scores[r * sub:(r + 1) * sub, :] for r in range(scores.shape[0] // sub)]
    for i, j in _sorting_network(len(blocks)):
        blocks[i], blocks[j] = jnp.maximum(blocks[i], blocks[j]), jnp.minimum(blocks[i], blocks[j])
    vals = []
    for r in range(k):
        m = jnp.max(blocks[0], axis=0, keepdims=True)
        vals.append(m)
        if r + 1 < k:
            hit = blocks[0] == m
            for d in range(k - 1 - r):
                blocks[d] = jnp.where(hit, blocks[d + 1], blocks[d])
    return vals


def _bf16_pair(v):
    bits = lax.bitcast_convert_type(v.astype(BF16).astype(F32), jnp.uint32)
    return lax.bitcast_convert_type(bits | (bits >> 16), F32)


def _peer_pre_body(x_ref, g_ref, wqt_ref, keys_ref, h2t_ref, ce_ref, rk_ref):
    topk = PEER_TOPK
    half = PEER_DKEY // 2
    h2 = _rms(x_ref[...], g_ref[...])
    h2t = h2.T.astype(BF16)
    h2t_ref[...] = h2t
    qt = _dot(wqt_ref[...], h2t).astype(BF16)
    for h in range(PEER_HEADS):
        s1 = _dot(keys_ref[h, 0], qt[(2 * h) * half:(2 * h + 1) * half, :])
        s2 = _dot(keys_ref[h, 1], qt[(2 * h + 1) * half:(2 * h + 2) * half, :])
        a = _top_values_sorted(s1, topk)
        b = _top_values_sorted(s2, topk)
        rank2 = jnp.zeros_like(s2)
        for q in range(topk):
            rank2 = rank2 + jnp.where(b[q] > s2, 1.0, 0.0)
        cands = [a[p] + b[q] for p in range(topk) for q in range(topk // (p + 1))]
        pad = (-len(cands)) % 8
        cand = jnp.concatenate(cands + [jnp.full_like(a[0], NEG_INF)] * pad, axis=0)
        tau = _top_values(cand, topk)[topk - 1]
        b_sorted = jnp.concatenate(b, axis=0)
        eb_sorted = jnp.exp(b_sorted - b[0])
        zsum = jnp.zeros_like(tau)
        cnt1 = jnp.zeros_like(s1)
        for p in range(topk):
            sel = (a[p] + b_sorted) >= tau
            cnt_p = jnp.sum(jnp.where(sel, 1.0, 0.0), axis=0, keepdims=True)
            zsum = zsum + jnp.exp(a[p] - a[0]) * jnp.sum(jnp.where(sel, eb_sorted, 0.0), axis=0, keepdims=True)
            cnt1 = jnp.where(s1 == a[p], cnt_p, cnt1)
        ce_ref[0, h] = _bf16_pair(cnt1)
        ce_ref[1, h] = _bf16_pair(jnp.exp(s1 - a[0]))
        rk_ref[0, h] = rank2.astype(BF16)
        rk_ref[1, h] = (jnp.exp(s2 - b[0]) * (0.5 / zsum)).astype(BF16)


def _peer_pre(x2, g, wqt, keys, tb):
    t = x2.shape[0]
    info = lambda: pl.BlockSpec((2, PEER_HEADS, PEER_NKEYS, tb), lambda i: (0, 0, 0, i))
    return pl.pallas_call(
        _peer_pre_body,
        grid=(t // tb,),
        in_specs=[pl.BlockSpec((tb, D_MODEL), lambda i: (i, 0)),
                  pl.BlockSpec((1, D_MODEL), lambda i: (0, 0)),
                  pl.BlockSpec((D_MODEL, D_MODEL), lambda i: (0, 0)),
                  pl.BlockSpec((PEER_HEADS, 2, PEER_NKEYS, PEER_DKEY // 2), lambda i: (0, 0, 0, 0))],
        out_specs=[pl.BlockSpec((D_MODEL, tb), lambda i: (0, i)), info(), info()],
        out_shape=[jax.ShapeDtypeStruct((D_MODEL, t), BF16),
                   jax.ShapeDtypeStruct((2, PEER_HEADS, PEER_NKEYS, t), F32),
                   jax.ShapeDtypeStruct((2, PEER_HEADS, PEER_NKEYS, t), BF16)],
        compiler_params=_params(("parallel",), VMEM_LIMIT),
        name="peer_pre",
    )(x2, g, wqt, keys)


def _peer_main_body(h2t_ref, ce_ref, rk_ref, u_ref, vt_ref, x_ref, o_ref, acc_ref, at_ref, pt_ref, *, eb, stage):
    e = pl.program_id(1)
    n_stage = eb // stage
    keys_per_stage = stage // PEER_NKEYS
    tb = h2t_ref.shape[1]
    zero = jnp.zeros((PEER_NKEYS, tb), BF16)

    @pl.when(e == 0)
    def _():
        acc_ref[...] = jnp.zeros(acc_ref.shape, F32)

    def row_tile(kind, h, i):
        row = ce_ref[kind, h, pl.ds(i, 1), :]
        return pltpu.bitcast(jnp.broadcast_to(row, (PEER_NKEYS // 2, tb)), BF16)

    def activations(s):
        at_ref[s % 2] = _dot(u_ref[s * stage:(s + 1) * stage, :], h2t_ref[...])

    def weighting(s):
        for k in range(keys_per_stage):
            i = (e * n_stage + s) * keys_per_stage + k
            rows = slice(k * PEER_NKEYS, (k + 1) * PEER_NKEYS)
            wd = None
            for h in range(PEER_HEADS):
                term = jnp.where(rk_ref[0, h] < row_tile(0, h, i), rk_ref[1, h], zero) * row_tile(1, h, i)
                wd = term if wd is None else wd + term
            at = at_ref[s % 2, rows, :]
            act = (at * (1.0 + lax.erf(at * math.sqrt(0.5)))).astype(BF16)
            pt_ref[(s // 2) % 2, (s % 2) * stage + k * PEER_NKEYS:(s % 2) * stage + (k + 1) * PEER_NKEYS, :] = wd * act

    def projection(p):
        acc_ref[...] += _dot(vt_ref[:, 2 * p * stage:2 * (p + 1) * stage], pt_ref[p % 2])

    activations(0)
    for s in range(n_stage):
        if s + 1 < n_stage:
            activations(s + 1)
        weighting(s)
        if s % 2 == 0 and s >= 2:
            projection(s // 2 - 1)
    projection(n_stage // 2 - 1)

    @pl.when(e == pl.num_programs(1) - 1)
    def _():
        o_ref[...] = x_ref[...] + acc_ref[...].T


def _peer_main(x2, h2t, ce, rk, u_b, vt_b, tb, eb):
    t = x2.shape[0]
    info = lambda: pl.BlockSpec((2, PEER_HEADS, PEER_NKEYS, tb), lambda i, e: (0, 0, 0, i))
    stage = 2 * PEER_NKEYS
    assert eb % (2 * stage) == 0
    return pl.pallas_call(
        functools.partial(_peer_main_body, eb=eb, stage=stage),
        grid=(t // tb, PEER_EXPERTS // eb),
        in_specs=[pl.BlockSpec((D_MODEL, tb), lambda i, e: (0, i)), info(), info(),
                  pl.BlockSpec((eb, D_MODEL), lambda i, e: (e, 0)),
                  pl.BlockSpec((D_MODEL, eb), lambda i, e: (0, e)),
                  pl.BlockSpec((tb, D_MODEL), lambda i, e: (i, 0))],
        out_specs=pl.BlockSpec((tb, D_MODEL), lambda i, e: (i, 0)),
        out_shape=jax.ShapeDtypeStruct((t, D_MODEL), F32),
        scratch_shapes=[pltpu.VMEM((D_MODEL, tb), F32), pltpu.VMEM((2, stage, tb), F32),
                        pltpu.VMEM((2, 2 * stage, tb), BF16)],
        compiler_params=_params(("parallel", "arbitrary"), VMEM_LIMIT),
        name="peer_main",
    )(h2t, ce, rk, u_b, vt_b, x2)


def _pick(n, pref):
    return pref if n % pref == 0 else n


def _permute_w_in(w):
    pts = [0]
    for width in IN_SPLITS:
        pts.append(pts[-1] + width)
    z, xbc, dtw, q, k, v, qm = [w[:, pts[n]:pts[n + 1]] for n in range(len(IN_SPLITS))]
    dtw = jnp.pad(dtw, ((0, 0), (0, DT_PAD - SSD_HEADS)))
    return jnp.concatenate([z, xbc, dtw, q, k, v, qm], axis=1).astype(BF16)


def _pad_row(v, width):
    return jnp.pad(v, (0, width - v.shape[0])).reshape(1, width)


def kernel(x, mem, norm1_g, w_in, conv_w, conv_b, dt_bias, a_log, d_skip, ssd_norm_g, sb_out_g, mem_norm_g,
           w_mem_kv, mq_norm_g, mk_norm_g, mem_out_g, w_out, norm2_g, peer_wq, peer_keys, peer_u, peer_v):
    bsz, seqlen, _ = x.shape
    t = bsz * seqlen
    depth = w_in.shape[0]
    tm = _pick(t, 512)
    tq = _pick(seqlen, 512)
    tb = _pick(t, 512)
    eb = 2048
    row = lambda v: v.reshape(1, -1)
    x2 = x.reshape(t, D_MODEL)
    for l in range(depth):
        u_ssd, u_sb, u_qm = _in_proj(x2, row(norm1_g[l]), _permute_w_in(w_in[l]), tm)
        y_ssd = _ssd(u_ssd.reshape(bsz, seqlen, SSD_OUT), conv_w[l], row(conv_b[l]),
                     _pad_row(dt_bias[l], DT_PAD), _pad_row(a_log[l], DT_PAD),
                     row(jnp.repeat(d_skip[l], HEAD_DIM)), row(ssd_norm_g[l]))
        y_sb = _sb_attn(u_sb.reshape(bsz, seqlen, SB_OUT), row(sb_out_g[l]), _pick(seqlen, 256))
        mk, mv = _mem_kv(mem, row(mem_norm_g[l]), w_mem_kv[l].astype(BF16),
                         row(jnp.tile(mk_norm_g[l], MEM_HEADS)))
        y_mem = _mem_attn(u_qm.reshape(bsz, seqlen, MEM_WIDTH), mk, mv,
                          row(jnp.tile(mq_norm_g[l], MEM_HEADS)), row(mem_out_g[l]), tq)
        x2 = _out_proj(x2, y_ssd.reshape(t, SSD_WIDTH), y_sb.reshape(t, SB_WIDTH),
                       y_mem.reshape(t, MEM_WIDTH), w_out[l].astype(BF16), tm)
        h2t, ce, rk = _peer_pre(x2, row(norm2_g[l]), peer_wq[l].T.astype(BF16), peer_keys[l].astype(BF16), tb)
        x2 = _peer_main(x2, h2t, ce, rk, peer_u[l].astype(BF16), peer_v[l].T.astype(BF16), tb, eb)
    return x2.reshape(bsz, seqlen, D_MODEL)
```

```python
import functools
import math

import jax
import jax.numpy as jnp
from jax import lax
from jax.experimental import pallas as pl
from jax.experimental.pallas import tpu as pltpu

F32 = jnp.float32
BF16 = jnp.bfloat16

D_MODEL = 1024
HEAD_DIM = 64
SSD_HEADS = 6
SSD_WIDTH = SSD_HEADS * HEAD_DIM
SSD_GROUPS = 2
SSD_STATE = 64
SSD_CONV = 4
SSD_CHUNK = 128
SSD_CONV_CH = SSD_WIDTH + 2 * SSD_GROUPS * SSD_STATE
SB_HEADS = 6
SB_WIDTH = SB_HEADS * HEAD_DIM
SB_BLOCK = 128
MEM_HEADS = 4
MEM_WIDTH = MEM_HEADS * HEAD_DIM
IN_SPLITS = [SSD_WIDTH, SSD_CONV_CH, SSD_HEADS, SB_WIDTH, SB_WIDTH, SB_WIDTH, MEM_WIDTH]
PEER_HEADS = 8
PEER_NKEYS = 128
PEER_EXPERTS = PEER_NKEYS * PEER_NKEYS
PEER_DKEY = 128
PEER_TOPK = 16
EPS = 1e-6

LANES = 128
DT_PAD = LANES
SSD_OUT = SSD_WIDTH + SSD_CONV_CH + DT_PAD
SB_OUT = 3 * SB_WIDTH
IN_COLS = SSD_OUT + SB_OUT + MEM_WIDTH
VMEM_LIMIT = 48 * 1024 * 1024

NEG_INF = float("-inf")
SB_EXP_ZERO = -104.0


def _params(sem, vmem=None, flags=None):
    return pltpu.CompilerParams(dimension_semantics=sem, vmem_limit_bytes=vmem, flags=flags)


def _rms(x, g):
    ms = jnp.mean(x * x, axis=-1, keepdims=True)
    return x * lax.rsqrt(ms + EPS) * g


def _sigmoid(x):
    return 1.0 / (1.0 + jnp.exp(-x))


def _dot(a, b):
    return jnp.dot(a, b, preferred_element_type=F32)


def _dot_nt(a, b):
    return lax.dot_general(a, b, (((1,), (1,)), ((), ())), preferred_element_type=F32)


def _split3(x):
    hi = x.astype(BF16)
    r1 = x - hi.astype(F32)
    mid = r1.astype(BF16)
    lo = (r1 - mid.astype(F32)).astype(BF16)
    return hi, mid, lo


def _inproj_body(x_ref, g_ref, w_ref, ssd_ref, sb_ref, qm_ref):
    h = _rms(x_ref[...], g_ref[...]).astype(BF16)
    ssd_ref[...] = _dot(h, w_ref[:, 0:SSD_OUT])
    sb_ref[...] = _dot(h, w_ref[:, SSD_OUT:SSD_OUT + SB_OUT]).astype(BF16)
    qm_ref[...] = _dot(h, w_ref[:, SSD_OUT + SB_OUT:IN_COLS])


def _in_proj(x2, g, w, tm):
    t = x2.shape[0]
    return pl.pallas_call(
        _inproj_body,
        grid=(t // tm,),
        in_specs=[pl.BlockSpec((tm, D_MODEL), lambda i: (i, 0)),
                  pl.BlockSpec((1, D_MODEL), lambda i: (0, 0)),
                  pl.BlockSpec((D_MODEL, IN_COLS), lambda i: (0, 0))],
        out_specs=[pl.BlockSpec((tm, SSD_OUT), lambda i: (i, 0)),
                   pl.BlockSpec((tm, SB_OUT), lambda i: (i, 0)),
                   pl.BlockSpec((tm, MEM_WIDTH), lambda i: (i, 0))],
        out_shape=[jax.ShapeDtypeStruct((t, SSD_OUT), F32),
                   jax.ShapeDtypeStruct((t, SB_OUT), BF16),
                   jax.ShapeDtypeStruct((t, MEM_WIDTH), F32)],
        compiler_params=_params(("parallel",), VMEM_LIMIT),
        name="in_proj",
    )(x2, g, w)


def _ssd_body(u_ref, cw_ref, cb_ref, dtb_ref, alog_ref, dsk_ref, ng_ref, y_ref, xpad_ref, st_ref):
    q_len = SSD_CHUNK
    c = pl.program_id(1)

    @pl.when(c == 0)
    def _():
        xpad_ref[0:8, :] = jnp.zeros((8, SSD_CONV_CH), F32)
        st_ref[...] = jnp.zeros(st_ref.shape, F32)

    xpad_ref[8:8 + q_len, :] = u_ref[:, SSD_WIDTH:SSD_WIDTH + SSD_CONV_CH]
    conv = cb_ref[...]
    for k in range(SSD_CONV):
        off = 8 - (SSD_CONV - 1) + k
        conv = conv + cw_ref[k:k + 1, :] * xpad_ref[off:off + q_len, :]
    xpad_ref[0:8, :] = xpad_ref[q_len:q_len + 8, :]
    xbc = conv * _sigmoid(conv)

    xs = xbc[:, 0:SSD_WIDTH]
    bm = xbc[:, SSD_WIDTH:SSD_WIDTH + LANES]
    cm = xbc[:, SSD_WIDTH + LANES:SSD_WIDTH + 2 * LANES]

    udt = u_ref[:, SSD_WIDTH + SSD_CONV_CH:SSD_OUT] + dtb_ref[...]
    dt = jnp.maximum(udt, 0.0) + jnp.log1p(jnp.exp(-jnp.abs(udt)))
    adt = dt * (-jnp.exp(alog_ref[...]))

    row = lax.broadcasted_iota(jnp.int32, (q_len, q_len), 0)
    lane = lax.broadcasted_iota(jnp.int32, (q_len, q_len), 1)
    causal = row >= lane
    lo_half = lane < HEAD_DIM
    tri = jnp.where(causal, 1.0, 0.0).astype(BF16)
    hi, mid, lo = _split3(adt)
    acs = _dot(tri, hi) + _dot(tri, mid) + _dot(tri, lo)
    acs_t = acs.T
    last = acs[q_len - 1:q_len, :]
    dec_s = jnp.exp(last - acs)
    eacs = jnp.exp(acs)
    cdec = jnp.exp(last)

    bm_b = bm.astype(BF16)
    bt_b = bm.T.astype(BF16)
    cmask = [jnp.where(lo_half, cm, 0.0).astype(BF16), jnp.where(lo_half, 0.0, cm).astype(BF16)]
    cb = [_dot_nt(cmask[g], bm_b) for g in range(SSD_GROUPS)]

    def col_pair(m, h0, h1):
        return jnp.where(lo_half, m[:, h0:h0 + 1], m[:, h1:h1 + 1])

    def lmat(h):
        seg = acs[:, h:h + 1] - acs_t[h:h + 1, :]
        return jnp.exp(jnp.where(causal, seg, NEG_INF))

    heads_per_group = SSD_HEADS // SSD_GROUPS
    ys = []
    for p in range(SSD_HEADS // 2):
        h0, h1 = 2 * p, 2 * p + 1
        g0, g1 = h0 // heads_per_group, h1 // heads_per_group
        xp = xs[:, p * LANES:(p + 1) * LANES]
        xdt = xp * col_pair(dt, h0, h1)
        xdt_b = xdt.astype(BF16)
        m0 = (cb[g0] * lmat(h0)).astype(BF16)
        m1 = (cb[g1] * lmat(h1)).astype(BF16)
        y_diag = jnp.where(lo_half, _dot(m0, xdt_b), _dot(m1, xdt_b))
        st = st_ref[p]
        st_b = st.astype(BF16)
        if g0 == g1:
            y_off = _dot(cmask[g0], st_b)
        else:
            y_off = jnp.where(lo_half, _dot(cmask[g0], st_b), _dot(cmask[g1], st_b))
        y_off = y_off * col_pair(eacs, h0, h1)
        dx = (xdt * col_pair(dec_s, h0, h1)).astype(BF16)
        st_ref[p] = st * col_pair(cdec, h0, h1) + _dot(bt_b, dx)
        ys.append(y_diag + y_off + xp * dsk_ref[:, p * LANES:(p + 1) * LANES])
    y = jnp.concatenate(ys, axis=1)
    z = u_ref[:, 0:SSD_WIDTH]
    y_ref[...] = _rms(y * (z * _sigmoid(z)), ng_ref[...]).astype(BF16)


def _ssd(u3, conv_w, conv_b, dt_bias, a_log, d_skip_cols, norm_g):
    b, s, _ = u3.shape
    nc = s // SSD_CHUNK
    const = lambda shape: pl.BlockSpec(shape, lambda i, j: (0,) * len(shape))
    return pl.pallas_call(
        _ssd_body,
        grid=(b, nc),
        in_specs=[pl.BlockSpec((None, SSD_CHUNK, SSD_OUT), lambda i, j: (i, j, 0)),
                  const((SSD_CONV, SSD_CONV_CH)), const((1, SSD_CONV_CH)),
                  const((1, DT_PAD)), const((1, DT_PAD)),
                  const((1, SSD_WIDTH)), const((1, SSD_WIDTH))],
        out_specs=pl.BlockSpec((None, SSD_CHUNK, SSD_WIDTH), lambda i, j: (i, j, 0)),
        out_shape=jax.ShapeDtypeStruct((b, s, SSD_WIDTH), BF16),
        scratch_shapes=[pltpu.VMEM((SSD_CHUNK + 8, SSD_CONV_CH), F32),
                        pltpu.VMEM((SSD_HEADS // 2, LANES, LANES), F32)],
        compiler_params=_params(("parallel", "arbitrary")),
        name="ssd",
    )(u3, conv_w, conv_b, dt_bias, a_log, d_skip_cols, norm_g)


def _sb_body(q_ref, k_ref, v_ref, g_ref, o_ref, acc_ref, r_ref, z_ref, lb_ref, cs_ref, rs_ref, *, tile):
    m = pl.program_id(1)
    npair = SB_HEADS // 2
    row = lax.broadcasted_iota(jnp.int32, (tile, tile), 0)
    col = lax.broadcasted_iota(jnp.int32, (tile, tile), 1)
    strict = col < row
    upper = jnp.where(row > col, 1.0, 0.0).astype(BF16)
    ucat = jnp.concatenate([upper, upper], axis=0)
    lo_half = lax.broadcasted_iota(jnp.int32, (tile, LANES), 1) < HEAD_DIM
    qh = []
    for p in range(npair):
        q = q_ref[:, p * LANES:(p + 1) * LANES] * jnp.asarray(HEAD_DIM ** -0.5, BF16)
        zero = jnp.zeros_like(q)
        qh += [jnp.where(lo_half, q, zero), jnp.where(lo_half, zero, q)]
    acc_ref[...] = jnp.zeros(acc_ref.shape, F32)
    r_ref[...] = jnp.zeros(r_ref.shape, F32)

    def key_tile(j, diag):
        start = pl.multiple_of(j * tile, tile)

        def scores(h):
            kb = k_ref[pl.ds(start, tile), (h // 2) * LANES:(h // 2 + 1) * LANES]
            z_ref[h % 2] = _dot_nt(qh[h], kb)

        def log_terms(h):
            z = z_ref[h % 2]
            log_beta = jnp.minimum(z, 0.0) - jnp.log(1.0 + jnp.exp(-jnp.abs(z)))
            log_keep = log_beta - z
            if diag:
                log_keep = jnp.where(strict, log_keep, 0.0)
            hi = log_keep.astype(BF16)
            lo = (log_keep - hi.astype(F32)).astype(BF16)
            lb_ref[h % 2] = log_beta
            rs_ref[h % 2] = jnp.broadcast_to(jnp.sum(log_keep, axis=-1, keepdims=True), (tile, LANES))
            cs_ref[h % 2] = _dot(jnp.concatenate([hi, lo], axis=1), ucat)

        def weights(h):
            r = r_ref[h]
            w = jnp.exp(lb_ref[h % 2] + cs_ref[h % 2] + jnp.concatenate([r] * (tile // LANES), axis=1))
            if diag:
                w = jnp.where(strict, w, 0.0)
            r_ref[h] = r + rs_ref[h % 2]
            vb = v_ref[pl.ds(start, tile), (h // 2) * LANES:(h // 2 + 1) * LANES]
            out = _dot(w.astype(BF16), vb)
            keep = lo_half if h % 2 == 0 else jnp.logical_not(lo_half)
            acc_ref[h // 2] += jnp.where(keep, out, 0.0)

        scores(0)
        for h in range(SB_HEADS):
            if h + 1 < SB_HEADS:
                scores(h + 1)
            log_terms(h)
            if h >= 1:
                weights(h - 1)
        weights(SB_HEADS - 1)

    key_tile(m, True)

    def more_weight():
        return jnp.max(r_ref[...]) > SB_EXP_ZERO

    def body(carry):
        t, _ = carry
        key_tile(m - 1 - t, False)
        return t + 1, more_weight()

    lax.while_loop(lambda c: jnp.logical_and(c[0] < m, c[1]), body, (jnp.int32(0), more_weight()))
    y = jnp.concatenate([acc_ref[p] for p in range(npair)], axis=1)
    o_ref[...] = _rms(y, g_ref[...]).astype(BF16)


def _sb_attn(sb3, out_g, tile):
    b, s, _ = sb3.shape
    return pl.pallas_call(
        functools.partial(_sb_body, tile=tile),
        grid=(b, s // tile),
        in_specs=[pl.BlockSpec((None, tile, SB_WIDTH), lambda bi, i: (bi, i, 0)),
                  pl.BlockSpec((None, s, SB_WIDTH), lambda bi, i: (bi, 0, 1)),
                  pl.BlockSpec((None, s, SB_WIDTH), lambda bi, i: (bi, 0, 2)),
                  pl.BlockSpec((1, SB_WIDTH), lambda bi, i: (0, 0))],
        out_specs=pl.BlockSpec((None, tile, SB_WIDTH), lambda bi, i: (bi, i, 0)),
        out_shape=jax.ShapeDtypeStruct((b, s, SB_WIDTH), BF16),
        scratch_shapes=[pltpu.VMEM((SB_HEADS // 2, tile, LANES), F32),
                        pltpu.VMEM((SB_HEADS, tile, LANES), F32),
                        pltpu.VMEM((2, tile, tile), F32), pltpu.VMEM((2, tile, tile), F32),
                        pltpu.VMEM((2, tile, tile), F32), pltpu.VMEM((2, tile, LANES), F32)],
        compiler_params=_params(("parallel", "arbitrary"), VMEM_LIMIT),
        name="sb_attn",
    )(sb3, sb3, sb3, out_g)


def _head_rms(x, g):
    lane = lax.broadcasted_iota(jnp.int32, x.shape, 1)
    sq = x * x
    inv = jnp.zeros_like(x)
    for h in range(MEM_HEADS):
        in_head = (lane >= h * HEAD_DIM) & (lane < (h + 1) * HEAD_DIM)
        ms = jnp.sum(jnp.where(in_head, sq, 0.0), axis=-1, keepdims=True) * (1.0 / HEAD_DIM)
        inv = jnp.where(in_head, lax.rsqrt(ms + EPS), inv)
    return x * inv * g


def _memkv_body(m_ref, g_ref, w_ref, kg_ref, k_ref, v_ref):
    h = _rms(m_ref[...], g_ref[...]).astype(BF16)
    kv = _dot(h, w_ref[...])
    k_ref[...] = _head_rms(kv[:, 0:MEM_WIDTH], kg_ref[...]).astype(BF16)
    v_ref[...] = kv[:, MEM_WIDTH:2 * MEM_WIDTH].astype(BF16)


def _mem_kv(mem, g, w, kg_cols):
    b, m, _ = mem.shape
    return pl.pallas_call(
        _memkv_body,
        grid=(b,),
        in_specs=[pl.BlockSpec((None, m, D_MODEL), lambda i: (i, 0, 0)),
                  pl.BlockSpec((1, D_MODEL), lambda i: (0, 0)),
                  pl.BlockSpec((D_MODEL, 2 * MEM_WIDTH), lambda i: (0, 0)),
                  pl.BlockSpec((1, MEM_WIDTH), lambda i: (0, 0))],
        out_specs=[pl.BlockSpec((None, m, MEM_WIDTH), lambda i: (i, 0, 0)),
                   pl.BlockSpec((None, m, MEM_WIDTH), lambda i: (i, 0, 0))],
        out_shape=[jax.ShapeDtypeStruct((b, m, MEM_WIDTH), BF16),
                   jax.ShapeDtypeStruct((b, m, MEM_WIDTH), BF16)],
        compiler_params=_params(("parallel",)),
        name="mem_kv",
    )(mem, g, w, kg_cols)


def _memattn_body(q_ref, k_ref, v_ref, qg_ref, og_ref, o_ref):
    qn = _head_rms(q_ref[...], qg_ref[...]).astype(BF16)
    lane = lax.broadcasted_iota(jnp.int32, qn.shape, 1)
    k = k_ref[...]
    v = v_ref[...]
    out = jnp.zeros(qn.shape, F32)
    for h in range(MEM_HEADS):
        in_head = (lane >= h * HEAD_DIM) & (lane < (h + 1) * HEAD_DIM)
        s = _dot_nt(jnp.where(in_head, qn, jnp.zeros_like(qn)), k) * (HEAD_DIM ** -0.5)
        e = jnp.exp(s - jnp.max(s, axis=-1, keepdims=True))
        p = e / jnp.sum(e, axis=-1, keepdims=True)
        out = jnp.where(in_head, _dot(p.astype(BF16), v), out)
    o_ref[...] = _rms(out, og_ref[...]).astype(BF16)


def _mem_attn(qm3, mk, mv, qg_cols, og, tq):
    b, s, _ = qm3.shape
    m = mk.shape[1]
    return pl.pallas_call(
        _memattn_body,
        grid=(b, s // tq),
        in_specs=[pl.BlockSpec((None, tq, MEM_WIDTH), lambda i, j: (i, j, 0)),
                  pl.BlockSpec((None, m, MEM_WIDTH), lambda i, j: (i, 0, 0)),
                  pl.BlockSpec((None, m, MEM_WIDTH), lambda i, j: (i, 0, 0)),
                  pl.BlockSpec((1, MEM_WIDTH), lambda i, j: (0, 0)),
                  pl.BlockSpec((1, MEM_WIDTH), lambda i, j: (0, 0))],
        out_specs=pl.BlockSpec((None, tq, MEM_WIDTH), lambda i, j: (i, j, 0)),
        out_shape=jax.ShapeDtypeStruct((b, s, MEM_WIDTH), BF16),
        compiler_params=_params(("parallel", "parallel")),
        name="mem_attn",
    )(qm3, mk, mv, qg_cols, og)


def _outproj_body(x_ref, ys_ref, yb_ref, ym_ref, w_ref, o_ref):
    acc = _dot(ys_ref[...], w_ref[0:SSD_WIDTH, :])
    acc = acc + _dot(yb_ref[...], w_ref[SSD_WIDTH:SSD_WIDTH + SB_WIDTH, :])
    acc = acc + _dot(ym_ref[...], w_ref[SSD_WIDTH + SB_WIDTH:D_MODEL, :])
    o_ref[...] = x_ref[...] + acc


def _out_proj(x2, y_ssd, y_sb, y_mem, w, tm):
    t = x2.shape[0]
    rows = lambda width: pl.BlockSpec((tm, width), lambda i: (i, 0))
    return pl.pallas_call(
        _outproj_body,
        grid=(t // tm,),
        in_specs=[rows(D_MODEL), rows(SSD_WIDTH), rows(SB_WIDTH), rows(MEM_WIDTH),
                  pl.BlockSpec((D_MODEL, D_MODEL), lambda i: (0, 0))],
        out_specs=rows(D_MODEL),
        out_shape=jax.ShapeDtypeStruct((t, D_MODEL), F32),
        compiler_params=_params(("parallel",), VMEM_LIMIT),
        name="out_proj",
    )(x2, y_ssd, y_sb, y_mem, w)


def _top_values(scores, k):
    work = scores
    vals = []
    for r in range(k):
        m = jnp.max(work, axis=0, keepdims=True)
        vals.append(m)
        if r + 1 < k:
            work = jnp.where(work == m, NEG_INF, work)
    return vals


def _sorting_network(n):
    pairs = []
    p = 1
    while p < n:
        k = p
        while k >= 1:
            for j in range(k % p, n - k, 2 * k):
                for i in range(min(k, n - j - k)):
                    if (i + j) // (2 * p) == (i + j + k) // (2 * p):
                        pairs.append((i + j, i + j + k))
            k //= 2
        p *= 2
    return pairs


def _top_values_sorted(scores, k):
    sub = 8
    blocks = [scores[r * sub:(r + 1) * sub, :] for r in range(scores.shape[0] // sub)]
    for i, j in _sorting_network(len(blocks)):
        blocks[i], blocks[j] = jnp.maximum(blocks[i], blocks[j]), jnp.minimum(blocks[i], blocks[j])
    vals = []
    for r in range(k):
        m = jnp.max(blocks[0], axis=0, keepdims=True)
        vals.append(m)
        if r + 1 < k:
            hit = blocks[0] == m
            for d in range(k - 1 - r):
                blocks[d] = jnp.where(hit, blocks[d + 1], blocks[d])
    return vals


def _bf16_pair(v):
    bits = lax.bitcast_convert_type(v.astype(BF16).astype(F32), jnp.uint32)
    return lax.bitcast_convert_type(bits | (bits >> 16), F32)


def _peer_pre_body(x_ref, g_ref, wqt_ref, keys_ref, h2t_ref, ce_ref, rk_ref):
    topk = PEER_TOPK
    half = PEER_DKEY // 2
    h2 = _rms(x_ref[...], g_ref[...])
    h2t = h2.T.astype(BF16)
    h2t_ref[...] = h2t
    qt = _dot(wqt_ref[...], h2t).astype(BF16)
    for h in range(PEER_HEADS):
        s1 = _dot(keys_ref[h, 0], qt[(2 * h) * half:(2 * h + 1) * half, :])
        s2 = _dot(keys_ref[h, 1], qt[(2 * h + 1) * half:(2 * h + 2) * half, :])
        a = _top_values_sorted(s1, topk)
        b = _top_values_sorted(s2, topk)
        rank2 = jnp.zeros_like(s2)
        for q in range(topk):
            rank2 = rank2 + jnp.where(b[q] > s2, 1.0, 0.0)
        cands = [a[p] + b[q] for p in range(topk) for q in range(topk // (p + 1))]
        pad = (-len(cands)) % 8
        cand = jnp.concatenate(cands + [jnp.full_like(a[0], NEG_INF)] * pad, axis=0)
        tau = _top_values(cand, topk)[topk - 1]
        b_sorted = jnp.concatenate(b, axis=0)
        eb_sorted = jnp.exp(b_sorted - b[0])
        zsum = jnp.zeros_like(tau)
        cnt1 = jnp.zeros_like(s1)
        for p in range(topk):
            sel = (a[p] + b_sorted) >= tau
            cnt_p = jnp.sum(jnp.where(sel, 1.0, 0.0), axis=0, keepdims=True)
            zsum = zsum + jnp.exp(a[p] - a[0]) * jnp.sum(jnp.where(sel, eb_sorted, 0.0), axis=0, keepdims=True)
            cnt1 = jnp.where(s1 == a[p], cnt_p, cnt1)
        ce_ref[0, h] = _bf16_pair(cnt1)
        ce_ref[1, h] = _bf16_pair(jnp.exp(s1 - a[0]))
        rk_ref[0, h] = rank2.astype(BF16)
        rk_ref[1, h] = (jnp.exp(s2 - b[0]) * (0.5 / zsum)).astype(BF16)


def _peer_pre(x2, g, wqt, keys, tb):
    t = x2.shape[0]
    info = lambda: pl.BlockSpec((2, PEER_HEADS, PEER_NKEYS, tb), lambda i: (0, 0, 0, i))
    return pl.pallas_call(
        _peer_pre_body,
        grid=(t // tb,),
        in_specs=[pl.BlockSpec((tb, D_MODEL), lambda i: (i, 0)),
                  pl.BlockSpec((1, D_MODEL), lambda i: (0, 0)),
                  pl.BlockSpec((D_MODEL, D_MODEL), lambda i: (0, 0)),
                  pl.BlockSpec((PEER_HEADS, 2, PEER_NKEYS, PEER_DKEY // 2), lambda i: (0, 0, 0, 0))],
        out_specs=[pl.BlockSpec((D_MODEL, tb), lambda i: (0, i)), info(), info()],
        out_shape=[jax.ShapeDtypeStruct((D_MODEL, t), BF16),
                   jax.ShapeDtypeStruct((2, PEER_HEADS, PEER_NKEYS, t), F32),
                   jax.ShapeDtypeStruct((2, PEER_HEADS, PEER_NKEYS, t), BF16)],
        compiler_params=_params(("parallel",), VMEM_LIMIT),
        name="peer_pre",
    )(x2, g, wqt, keys)


def _peer_main_body(h2t_ref, ce_ref, rk_ref, u_ref, vt_ref, x_ref, o_ref, acc_ref, at_ref, pt_ref, *, eb, stage):
    e = pl.program_id(1)
    n_stage = eb // stage
    keys_per_stage = stage // PEER_NKEYS
    tb = h2t_ref.shape[1]
    zero = jnp.zeros((PEER_NKEYS, tb), BF16)

    @pl.when(e == 0)
    def _():
        acc_ref[...] = jnp.zeros(acc_ref.shape, F32)

    def row_tile(kind, h, i):
        row = ce_ref[kind, h, pl.ds(i, 1), :]
        return pltpu.bitcast(jnp.broadcast_to(row, (PEER_NKEYS // 2, tb)), BF16)

    def activations(s):
        at_ref[s % 2] = _dot(u_ref[s * stage:(s + 1) * stage, :], h2t_ref[...])

    def weighting(s):
        for k in range(keys_per_stage):
            i = (e * n_stage + s) * keys_per_stage + k
            rows = slice(k * PEER_NKEYS, (k + 1) * PEER_NKEYS)
            out_rows = slice((s % 2) * stage + k * PEER_NKEYS, (s % 2) * stage + (k + 1) * PEER_NKEYS)
            wd = None
            for h in range(PEER_HEADS):
                term = jnp.where(rk_ref[0, h] < row_tile(0, h, i), rk_ref[1, h], zero) * row_tile(1, h, i)
                wd = term if wd is None else wd + term
            at = at_ref[s % 2, rows, :]
            act = (at * (1.0 + lax.erf(at * math.sqrt(0.5)))).astype(BF16)
            pt_ref[(s // 2) % 2, out_rows, :] = wd * act

    def projection(p):
        acc_ref[...] += _dot(vt_ref[:, 2 * p * stage:2 * (p + 1) * stage], pt_ref[p % 2])

    activations(0)
    for s in range(n_stage):
        if s + 1 < n_stage:
            activations(s + 1)
        weighting(s)
        if s % 2 == 0 and s >= 2:
            projection(s // 2 - 1)
    projection(n_stage // 2 - 1)

    @pl.when(e == pl.num_programs(1) - 1)
    def _():
        o_ref[...] = x_ref[...] + acc_ref[...].T


def _peer_main(x2, h2t, ce, rk, u_b, vt_b, tb, eb):
    t = x2.shape[0]
    info = lambda: pl.BlockSpec((2, PEER_HEADS, PEER_NKEYS, tb), lambda i, e: (0, 0, 0, i))
    stage = 4 * PEER_NKEYS
    assert eb % (2 * stage) == 0
    return pl.pallas_call(
        functools.partial(_peer_main_body, eb=eb, stage=stage),
        grid=(t // tb, PEER_EXPERTS // eb),
        in_specs=[pl.BlockSpec((D_MODEL, tb), lambda i, e: (0, i)), info(), info(),
                  pl.BlockSpec((eb, D_MODEL), lambda i, e: (e, 0)),
                  pl.BlockSpec((D_MODEL, eb), lambda i, e: (0, e)),
                  pl.BlockSpec((tb, D_MODEL), lambda i, e: (i, 0))],
        out_specs=pl.BlockSpec((tb, D_MODEL), lambda i, e: (i, 0)),
        out_shape=jax.ShapeDtypeStruct((t, D_MODEL), F32),
        scratch_shapes=[pltpu.VMEM((D_MODEL, tb), F32), pltpu.VMEM((2, stage, tb), F32),
                        pltpu.VMEM((2, 2 * stage, tb), BF16)],
        compiler_params=_params(("parallel", "arbitrary"), VMEM_LIMIT),
        name="peer_main",
    )(h2t, ce, rk, u_b, vt_b, x2)


def _pick(n, pref):
    return pref if n % pref == 0 else n


def _permute_w_in(w):
    pts = [0]
    for width in IN_SPLITS:
        pts.append(pts[-1] + width)
    z, xbc, dtw, q, k, v, qm = [w[:, pts[n]:pts[n + 1]] for n in range(len(IN_SPLITS))]
    dtw = jnp.pad(dtw, ((0, 0), (0, DT_PAD - SSD_HEADS)))
    return jnp.concatenate([z, xbc, dtw, q, k, v, qm], axis=1).astype(BF16)


def _pad_row(v, width):
    return jnp.pad(v, (0, width - v.shape[0])).reshape(1, width)


def kernel(x, mem, norm1_g, w_in, conv_w, conv_b, dt_bias, a_log, d_skip, ssd_norm_g, sb_out_g, mem_norm_g,
           w_mem_kv, mq_norm_g, mk_norm_g, mem_out_g, w_out, norm2_g, peer_wq, peer_keys, peer_u, peer_v):
    bsz, seqlen, _ = x.shape
    t = bsz * seqlen
    depth = w_in.shape[0]
    tm = _pick(t, 512)
    tq = _pick(seqlen, 512)
    tb = _pick(t, 512)
    eb = 2048
    row = lambda v: v.reshape(1, -1)
    x2 = x.reshape(t, D_MODEL)
    for l in range(depth):
        u_ssd, u_sb, u_qm = _in_proj(x2, row(norm1_g[l]), _permute_w_in(w_in[l]), tm)
        y_ssd = _ssd(u_ssd.reshape(bsz, seqlen, SSD_OUT), conv_w[l], row(conv_b[l]),
                     _pad_row(dt_bias[l], DT_PAD), _pad_row(a_log[l], DT_PAD),
                     row(jnp.repeat(d_skip[l], HEAD_DIM)), row(ssd_norm_g[l]))
        y_sb = _sb_attn(u_sb.reshape(bsz, seqlen, SB_OUT), row(sb_out_g[l]), _pick(seqlen, 256))
        mk, mv = _mem_kv(mem, row(mem_norm_g[l]), w_mem_kv[l].astype(BF16),
                         row(jnp.tile(mk_norm_g[l], MEM_HEADS)))
        y_mem = _mem_attn(u_qm.reshape(bsz, seqlen, MEM_WIDTH), mk, mv,
                          row(jnp.tile(mq_norm_g[l], MEM_HEADS)), row(mem_out_g[l]), tq)
        x2 = _out_proj(x2, y_ssd.reshape(t, SSD_WIDTH), y_sb.reshape(t, SB_WIDTH),
                       y_mem.reshape(t, MEM_WIDTH), w_out[l].astype(BF16), tm)
        h2t, ce, rk = _peer_pre(x2, row(norm2_g[l]), peer_wq[l].T.astype(BF16), peer_keys[l].astype(BF16), tb)
        x2 = _peer_main(x2, h2t, ce, rk, peer_u[l].astype(BF16), peer_v[l].T.astype(BF16), tb, eb)
    return x2.reshape(bsz, seqlen, D_MODEL)
```
